```python
import jax, jax.numpy as jnp
from jax import lax
import numpy as np

D_MODEL = 2048
BATCH = 8
SEQ = 2048
DEPTH = 4
DEC_BATCH = 16
DEC_SEQ = 16
PAST_LEN = 1024

CHUNK = 64
N_EVEN = (DEPTH + 1) // 2
N_ODD = DEPTH // 2
EPS = 1e-6
A_HEADS = 4
A_WIDTH = D_MODEL
A_HEAD_DIM = A_WIDTH // A_HEADS
GMLP_CHUNK = 128
B_HEADS = 8
B_DK = D_MODEL // B_HEADS
B_DV = D_MODEL // B_HEADS
B_WIDTH = B_HEADS * B_DV
ROPE_BASE = 10000.0
C_HEADS = 8
C_DK = D_MODEL // C_HEADS
C_DV = 2 * C_DK
C_WIDTH = C_HEADS * C_DV
GLA_RANK = 16
GLA_GATE_NORM = 16.0
GLA_BLOCK = 16
E_SPLITS = [A_WIDTH, A_WIDTH, A_WIDTH, B_HEADS * B_DK, B_HEADS * B_DK, B_WIDTH, B_WIDTH]
E_IN = sum(E_SPLITS)
E_MIX = A_WIDTH + B_WIDTH
O_SPLITS = [C_HEADS * C_DK, C_HEADS * C_DK, C_WIDTH, C_WIDTH, GLA_RANK]
O_IN = sum(O_SPLITS)
O_MIX = C_WIDTH

kernel_name = 'streaming_gmlp_retnet_gla_adaln'


def rms_norm(x, g):
    xf = x.astype(jnp.float32)
    y = xf * lax.rsqrt(jnp.mean(xf * xf, axis=-1, keepdims=True) + EPS)
    return (y * g.astype(jnp.float32)).astype(x.dtype)


def head_layer_norm(x, g):
    xf = x.astype(jnp.float32)
    mu = jnp.mean(xf, axis=-1, keepdims=True)
    xc = xf - mu
    var = jnp.mean(xc * xc, axis=-1, keepdims=True)
    return (xc * lax.rsqrt(var + EPS) * g.astype(jnp.float32)).astype(x.dtype)


def rotary(x, pos):
    half = x.shape[-1] // 2
    freqs = ROPE_BASE ** (-jnp.arange(half, dtype=jnp.float32) / half)
    ang = pos.astype(jnp.float32)[:, None] * freqs[None, :]
    cos = jnp.cos(ang)[None, :, None, :]
    sin = jnp.sin(ang)[None, :, None, :]
    xf = x.astype(jnp.float32)
    x1, x2 = xf[..., :half], xf[..., half:]
    return jnp.concatenate([x1 * cos - x2 * sin, x1 * sin + x2 * cos], axis=-1).astype(x.dtype)


def run_blocks(block_fn, S0, xs, block):
    B, H, L = xs[0].shape[:3]
    nb = L // block
    xs_b = tuple(jnp.moveaxis(a.reshape(B, H, nb, block, a.shape[-1]), 2, 0) for a in xs)
    S, o = lax.scan(block_fn, S0, xs_b)
    o = jnp.moveaxis(o, 0, 2).reshape(B, H, L, o.shape[-1])
    return o, S


def retention_block(S, inputs):
    q, k, v = inputs
    L = q.shape[2]
    log_g = jnp.log1p(-jnp.exp2(-5.0 - jnp.arange(B_HEADS, dtype=jnp.float32)))
    t = jnp.arange(L, dtype=jnp.float32)
    diff = t[:, None] - t[None, :]
    decay = jnp.where(diff >= 0, jnp.exp(jnp.maximum(diff, 0.0)[None] * log_g[:, None, None]), 0.0).astype(v.dtype)
    q_dec = jnp.exp((t[None, :] + 1.0) * log_g[:, None]).astype(v.dtype)
    k_dec = jnp.exp((L - 1.0 - t[None, :]) * log_g[:, None]).astype(v.dtype)
    s_dec = jnp.exp(L * log_g).astype(S.dtype)
    scores = jnp.einsum('bhtd,bhsd->bhts', q, k) * decay
    o = (jnp.einsum('bhts,bhsv->bhtv', scores, v)
         + jnp.einsum('bhtd,bhdv->bhtv', q * q_dec[None, :, :, None], S.astype(v.dtype)))
    S_new = s_dec[None, :, None, None] * S + jnp.einsum('bhsd,bhsv->bhdv', k * k_dec[None, :, :, None], v).astype(S.dtype)
    return S_new.astype(S.dtype), o


def gla_block(S, inputs):
    q, k, v, gk = inputs
    L = q.shape[2]
    b = jnp.cumsum(gk, axis=2)
    b_end = b[:, :, -1:, :]
    qf = q.astype(jnp.float32) * jnp.exp(b)
    kf = k.astype(jnp.float32) * jnp.exp(-b)
    causal = jnp.tril(jnp.ones((L, L), dtype=bool))
    scores = jnp.where(causal, jnp.einsum('bhtd,bhsd->bhts', qf, kf), 0.0).astype(v.dtype)
    o = (jnp.einsum('bhts,bhsv->bhtv', scores, v)
         + jnp.einsum('bhtd,bhdv->bhtv', qf.astype(v.dtype), S.astype(v.dtype)))
    k_tail = (k.astype(jnp.float32) * jnp.exp(b_end - b)).astype(v.dtype)
    S_new = (jnp.exp(b_end[:, :, 0, :, None]).astype(S.dtype) * S
             + jnp.einsum('bhsd,bhsv->bhdv', k_tail, v).astype(S.dtype))
    return S_new.astype(S.dtype), o


def spatial_gating(u, v, w_s, b_s):
    B, L = v.shape[:2]
    seg = min(L, GMLP_CHUNK)
    nc = L // seg
    p = jnp.arange(seg)
    mask = (p[None, :] // CHUNK) <= (p[:, None] // CHUNK)
    w = jnp.where(mask[None], w_s[:, :seg, :seg], 0.0).astype(v.dtype)
    vc = v.reshape(B, nc, seg, A_HEADS, A_HEAD_DIM)
    mixed = jnp.einsum('hts,bcshd->bcthd', w, vc) + b_s[:, :seg].T[None, None, :, :, None].astype(v.dtype)
    return u * mixed.reshape(B, L, A_HEADS, A_HEAD_DIM)


def modulate(x, c, norm_g, w_mod, b_mod):
    mod = jax.nn.silu(c) @ w_mod + b_mod
    shift, scale, gate = jnp.split(mod, 3, axis=-1)
    h = rms_norm(x, norm_g) * (1 + scale[:, None, :]) + shift[:, None, :]
    return h, gate[:, None, :]


def even_mixer(h, pos, S0, block, w_in, ln_g, w_s, b_s, ret_g, w_out):
    B, L, _ = h.shape
    proj = h @ w_in
    u, va, za, q, k, vb, zb = jnp.split(proj, np.cumsum(E_SPLITS)[:-1].tolist(), axis=-1)
    u = jax.nn.gelu(u).reshape(B, L, A_HEADS, A_HEAD_DIM)
    va = head_layer_norm(jax.nn.gelu(va).reshape(B, L, A_HEADS, A_HEAD_DIM), ln_g.reshape(A_HEADS, A_HEAD_DIM))
    ya = spatial_gating(u, va, w_s, b_s).reshape(B, L, A_WIDTH) * jax.nn.silu(za)
    q = rotary(q.reshape(B, L, B_HEADS, B_DK), pos).transpose(0, 2, 1, 3)
    k = (rotary(k.reshape(B, L, B_HEADS, B_DK), pos) * (B_DK ** -0.5)).transpose(0, 2, 1, 3)
    vb = vb.reshape(B, L, B_HEADS, B_DV).transpose(0, 2, 1, 3)
    o, S = run_blocks(retention_block, S0, (q, k, vb), block)
    yb = rms_norm(o.transpose(0, 2, 1, 3), ret_g).reshape(B, L, B_WIDTH) * jax.nn.silu(zb)
    y = jnp.concatenate([ya, yb], axis=-1) @ w_out
    return y, S, va.reshape(B, L, A_WIDTH)


def odd_mixer(h, S0, block, w_in, w_gk2, b_gk, gla_g, w_out):
    B, L, _ = h.shape
    proj = h @ w_in
    q, k, v, z, lr = jnp.split(proj, np.cumsum(O_SPLITS)[:-1].tolist(), axis=-1)
    gk = jax.nn.log_sigmoid((lr @ w_gk2 + b_gk).astype(jnp.float32)) / GLA_GATE_NORM
    tr = lambda a, d: a.reshape(B, L, C_HEADS, d).transpose(0, 2, 1, 3)
    q = tr(q * (C_DK ** -0.5), C_DK)
    o, S = run_blocks(gla_block, S0, (q, tr(k, C_DK), tr(v, C_DV), tr(gk, C_DK)), block)
    y = rms_norm(o.transpose(0, 2, 1, 3), gla_g).reshape(B, L, C_WIDTH) * jax.nn.silu(z)
    return y @ w_out, S


def trunk(x, c, pos, ret_state, gla_state, ret_blk, gla_blk,
          norm_g, w_mod, b_mod, w_in_e, gmlp_ln_g, gmlp_ws, gmlp_bs, ret_norm_g, w_out_e,
          w_in_o, w_gk2, b_gk, gla_norm_g, w_out_o, final_g):
    ret_new, gla_new, v_rows = [], [], []
    for l in range(DEPTH):
        h, gate = modulate(x, c, norm_g[l], w_mod[l], b_mod[l])
        i = l // 2
        if l % 2 == 0:
            y, S, va = even_mixer(h, pos, ret_state[i], ret_blk, w_in_e[i], gmlp_ln_g[i], gmlp_ws[i],
                                  gmlp_bs[i], ret_norm_g[i], w_out_e[i])
            ret_new.append(S)
            v_rows.append(va)
        else:
            y, S = odd_mixer(h, gla_state[i], gla_blk, w_in_o[i], w_gk2[i], b_gk[i], gla_norm_g[i], w_out_o[i])
            gla_new.append(S)
        x = x + gate * y
    return rms_norm(x, final_g), jnp.stack(ret_new), jnp.stack(gla_new), jnp.stack(v_rows)


def setup_inputs(seed: int = 0) -> dict:
    key = jax.random.key(seed)
    ks = jax.random.split(key, 24)
    f32 = jnp.float32
    nrm = lambda k, shape, s: jax.random.normal(k, shape, f32) * s
    return {
        'x_prompt': nrm(ks[0], (BATCH, SEQ, D_MODEL), 1.0),
        'x_sample': nrm(ks[1], (DEC_BATCH, DEC_SEQ, D_MODEL), 1.0),
        'state_ret': nrm(ks[2], (N_EVEN, DEC_BATCH, B_HEADS, B_DK, B_DV), B_DK ** -0.5),
        'state_gla': nrm(ks[3], (N_ODD, DEC_BATCH, C_HEADS, C_DK, C_DV), C_DK ** -0.5),
        'c_prompt': nrm(ks[4], (BATCH, D_MODEL), 1.0),
        'c_sample': nrm(ks[5], (DEC_BATCH, D_MODEL), 1.0),
        'norm_g': 1.0 + nrm(ks[6], (DEPTH, D_MODEL), 0.02),
        'w_mod': nrm(ks[7], (DEPTH, D_MODEL, 3 * D_MODEL), D_MODEL ** -0.5),
        'b_mod': nrm(ks[8], (DEPTH, 3 * D_MODEL), 0.02),
        'w_in_e': nrm(ks[9], (N_EVEN, D_MODEL, E_IN), D_MODEL ** -0.5),
        'gmlp_ln_g': 1.0 + nrm(ks[10], (N_EVEN, A_WIDTH), 0.02),
        'gmlp_ws': nrm(ks[11], (N_EVEN, A_HEADS, GMLP_CHUNK, GMLP_CHUNK), GMLP_CHUNK ** -0.5),
        'gmlp_bs': 1.0 + nrm(ks[12], (N_EVEN, A_HEADS, GMLP_CHUNK), 0.02),
        'ret_norm_g': 1.0 + nrm(ks[13], (N_EVEN, B_DV), 0.02),
        'w_out_e': nrm(ks[14], (N_EVEN, E_MIX, D_MODEL), E_MIX ** -0.5),
        'w_in_o': nrm(ks[15], (N_ODD, D_MODEL, O_IN), D_MODEL ** -0.5),
        'w_gk2': nrm(ks[16], (N_ODD, GLA_RANK, C_HEADS * C_DK), GLA_RANK ** -0.5),
        'b_gk': nrm(ks[17], (N_ODD, C_HEADS * C_DK), 0.02),
        'gla_norm_g': 1.0 + nrm(ks[18], (N_ODD, C_DV), 0.02),
        'w_out_o': nrm(ks[19], (N_ODD, O_MIX, D_MODEL), O_MIX ** -0.5),
        'final_g': 1.0 + nrm(ks[20], (D_MODEL,), 0.02),
    }


def reference(x_prompt, x_sample, state_ret, state_gla, c_prompt, c_sample,
              norm_g, w_mod, b_mod, w_in_e, gmlp_ln_g, gmlp_ws, gmlp_bs, ret_norm_g, w_out_e,
              w_in_o, w_gk2, b_gk, gla_norm_g, w_out_o, final_g):
    weights = (norm_g, w_mod, b_mod, w_in_e, gmlp_ln_g, gmlp_ws, gmlp_bs, ret_norm_g, w_out_e,
               w_in_o, w_gk2, b_gk, gla_norm_g, w_out_o, final_g)
    bp, lp = x_prompt.shape[:2]
    ls = x_sample.shape[1]
    ret0 = jnp.zeros((N_EVEN, bp, B_HEADS, B_DK, B_DV), x_prompt.dtype)
    gla0 = jnp.zeros((N_ODD, bp, C_HEADS, C_DK, C_DV), x_prompt.dtype)
    y_prompt, ret_prompt, gla_prompt, _ = trunk(x_prompt, c_prompt, jnp.arange(lp), ret0, gla0,
                                                CHUNK, GLA_BLOCK, *weights)
    y_sample, ret_sample, gla_sample, gmlp_v_sample = trunk(x_sample, c_sample, PAST_LEN + jnp.arange(ls),
                                                            state_ret, state_gla, ls, ls, *weights)
    return (y_prompt, y_sample, ret_prompt, ret_sample, gla_prompt, gla_sample, gmlp_v_sample)
```

```python
import functools
import math

import jax
import jax.numpy as jnp
import numpy as np
from jax import lax
from jax.experimental import pallas as pl
from jax.experimental.pallas import tpu as pltpu

F32 = jnp.float32
BF16 = jnp.bfloat16

D_MODEL = 2048
DEPTH = 4
EPS = 1e-6
CHUNK = 64
A_HEADS = 4
A_HEAD_DIM = 512
A_WIDTH = 2048
GMLP_CHUNK = 128
B_HEADS = 8
B_DK = 256
B_DV = 256
B_WIDTH = 2048
ROPE_BASE = 10000.0
C_HEADS = 8
C_DK = 256
C_DV = 512
C_WIDTH = 4096
GLA_RANK = 16
GLA_GATE_NORM = 16.0
E_IN = 3 * A_WIDTH + 2 * B_HEADS * B_DK + 2 * B_WIDTH
E_MIX = A_WIDTH + B_WIDTH
O_MAIN = 2 * C_HEADS * C_DK + 2 * C_WIDTH
LR_PAD = 128

VMEM_LIMIT_BYTES = 56 * 1024 * 1024


def _params(*sem):
    return pltpu.CompilerParams(dimension_semantics=sem, vmem_limit_bytes=VMEM_LIMIT_BYTES)


def _dot(a, b):
    return jnp.dot(a, b, preferred_element_type=F32)


def _dot_nt(a, b):
    return lax.dot_general(a, b, (((1,), (1,)), ((), ())), preferred_element_type=F32)


def _dot_tn(a, b):
    return lax.dot_general(a, b, (((0,), (0,)), ((), ())), preferred_element_type=F32)


def _silu(x):
    return x * jax.nn.sigmoid(x)


def _log_sigmoid(x):
    return jnp.minimum(x, 0.0) - jnp.log1p(jnp.exp(-jnp.abs(x)))


def _mod_kernel(c_ref, w_ref, b_ref, o_ref):
    c = c_ref[...]
    s = _silu(c).astype(BF16)
    o_ref[0] = _dot(s, w_ref[0].astype(BF16)) + b_ref[0]


def _modulation(c_all, w_mod, b_mod):
    rows = c_all.shape[0]
    tn = 1024
    n = 3 * D_MODEL
    return pl.pallas_call(
        _mod_kernel,
        grid=(DEPTH, n // tn),
        in_specs=[
            pl.BlockSpec((rows, D_MODEL), lambda l, j: (0, 0)),
            pl.BlockSpec((1, D_MODEL, tn), lambda l, j: (l, 0, j)),
            pl.BlockSpec((1, 1, tn), lambda l, j: (l, 0, j)),
        ],
        out_specs=pl.BlockSpec((1, rows, tn), lambda l, j: (l, 0, j)),
        out_shape=jax.ShapeDtypeStruct((DEPTH, rows, n), F32),
        compiler_params=_params("arbitrary", "arbitrary"),
        name="modulation",
    )(c_all, w_mod, b_mod.reshape(DEPTH, 1, n))


def _in_proj_kernel(x_ref, sc_ref, sh_ref, g_ref, w_ref, *rest, has_lr):
    if has_lr:
        wlr_ref, o_ref, lr_ref, h_scr = rest
    else:
        o_ref, h_scr = rest

    @pl.when(pl.program_id(1) == 0)
    def _():
        x = x_ref[...]
        y = x * lax.rsqrt(jnp.mean(x * x, axis=-1, keepdims=True) + EPS) * g_ref[...]
        h = (y * (1.0 + sc_ref[0]) + sh_ref[0]).astype(BF16)
        h_scr[...] = h
        if has_lr:
            lr_ref[...] = _dot(h, wlr_ref[...])

    o_ref[...] = _dot(h_scr[...], w_ref[...]).astype(o_ref.dtype)


def _in_proj(x, scale, shift, g, w, wlr, *, tm, tiles_per_group):
    t, d = x.shape
    n = w.shape[1]
    tn = 1024
    rm = scale.shape[1]
    has_lr = wlr is not None
    mod_spec = pl.BlockSpec((1, rm, d), lambda i, j: (i // tiles_per_group, 0, 0))
    in_specs = [
        pl.BlockSpec((tm, d), lambda i, j: (i, 0)),
        mod_spec, mod_spec,
        pl.BlockSpec((1, d), lambda i, j: (0, 0)),
        pl.BlockSpec((d, tn), lambda i, j: (0, j)),
    ]
    args = [x, scale, shift, g, w]
    out_specs = [pl.BlockSpec((tm, tn), lambda i, j: (i, j))]
    out_shape = [jax.ShapeDtypeStruct((t, n), BF16)]
    if has_lr:
        in_specs.append(pl.BlockSpec((d, LR_PAD), lambda i, j: (0, 0)))
        args.append(wlr)
        out_specs.append(pl.BlockSpec((tm, LR_PAD), lambda i, j: (i, 0)))
        out_shape.append(jax.ShapeDtypeStruct((t, LR_PAD), F32))
    return pl.pallas_call(
        functools.partial(_in_proj_kernel, has_lr=has_lr),
        grid=(t // tm, n // tn),
        in_specs=in_specs,
        out_specs=out_specs,
        out_shape=out_shape,
        scratch_shapes=[pltpu.VMEM((tm, d), BF16)],
        compiler_params=_params("arbitrary", "arbitrary"),
        name="in_proj",
    )(*args)


def _out_proj_kernel(y_ref, w_ref, x_ref, gate_ref, o_ref):
    o_ref[...] = x_ref[...] + gate_ref[0] * _dot(y_ref[...], w_ref[...])


def _out_proj(y, w, x, gate, *, tm, tiles_per_group):
    t, k = y.shape
    d = w.shape[1]
    tn = 512
    rm = gate.shape[1]
    return pl.pallas_call(
        _out_proj_kernel,
        grid=(t // tm, d // tn),
        in_specs=[
            pl.BlockSpec((tm, k), lambda i, j: (i, 0)),
            pl.BlockSpec((k, tn), lambda i, j: (0, j)),
            pl.BlockSpec((tm, tn), lambda i, j: (i, j)),
            pl.BlockSpec((1, rm, tn), lambda i, j: (i // tiles_per_group, 0, j)),
        ],
        out_specs=pl.BlockSpec((tm, tn), lambda i, j: (i, j)),
        out_shape=jax.ShapeDtypeStruct((t, d), F32),
        compiler_params=_params("arbitrary", "arbitrary"),
        name="out_proj",
    )(y, w, x, gate)


def _final_norm_kernel(x_ref, g_ref, o_ref):
    x = x_ref[...]
    o_ref[...] = x * lax.rsqrt(jnp.mean(x * x, axis=-1, keepdims=True) + EPS) * g_ref[...]


def _final_norm(x, g, *, tm):
    t, d = x.shape
    return pl.pallas_call(
        _final_norm_kernel,
        grid=(t // tm,),
        in_specs=[pl.BlockSpec((tm, d), lambda i: (i, 0)), pl.BlockSpec((1, d), lambda i: (0, 0))],
        out_specs=pl.BlockSpec((tm, d), lambda i: (i, 0)),
        out_shape=jax.ShapeDtypeStruct((t, d), F32),
        compiler_params=_params("arbitrary"),
        name="final_norm",
    )(x, g)


def _retention_log_gamma():
    return [math.log1p(-(2.0 ** (-5.0 - h))) for h in range(B_HEADS)]


def _even_mix_kernel(p_ref, cos_ref, sin_ref, wm_ref, bs_ref, lng_ref, retg_ref, *rest,
                     ts, seg, has_s0, emit_va):
    rest = list(rest)
    s0_ref = rest.pop(0) if has_s0 else None
    y_ref = rest.pop(0)
    sfin_ref = rest.pop(0)
    va_ref = rest.pop(0) if emit_va else None
    s_scr, dec_scr, qd_scr, kd_scr = rest

    b = pl.program_id(0)
    t = pl.program_id(1)
    nt = pl.num_programs(1)
    log_g = _retention_log_gamma()

    @pl.when((b == 0) & (t == 0))
    def _():
        row = lax.broadcasted_iota(jnp.int32, (ts, ts), 0).astype(F32)
        col = lax.broadcasted_iota(jnp.int32, (ts, ts), 1).astype(F32)
        diff = row - col
        rowd = lax.broadcasted_iota(jnp.int32, (ts, B_DK), 0).astype(F32)
        for h in range(B_HEADS):
            dec_scr[h] = jnp.where(diff >= 0, jnp.exp(jnp.maximum(diff, 0.0) * log_g[h]), 0.0)
            qd_scr[h] = jnp.exp((rowd + 1.0) * log_g[h])
            kd_scr[h] = jnp.exp((ts - 1.0 - rowd) * log_g[h])

    @pl.when(t == 0)
    def _():
        if has_s0:
            s_scr[...] = s0_ref[0]
        else:
            s_scr[...] = jnp.zeros_like(s_scr)

    for a in range(A_HEADS):
        c0 = a * A_HEAD_DIM
        g_ln = lng_ref[:, c0:c0 + A_HEAD_DIM]
        for s in range(ts // seg):
            rows = slice(s * seg, (s + 1) * seg)
            u = jax.nn.gelu(p_ref[rows, c0:c0 + A_HEAD_DIM].astype(F32))
            v = jax.nn.gelu(p_ref[rows, A_WIDTH + c0:A_WIDTH + c0 + A_HEAD_DIM].astype(F32))
            z = p_ref[rows, 2 * A_WIDTH + c0:2 * A_WIDTH + c0 + A_HEAD_DIM].astype(F32)
            mu = jnp.mean(v, axis=-1, keepdims=True)
            vc = v - mu
            var = jnp.mean(vc * vc, axis=-1, keepdims=True)
            vn = vc * lax.rsqrt(var + EPS) * g_ln
            if emit_va:
                va_ref[rows, c0:c0 + A_HEAD_DIM] = vn
            mixed = _dot(wm_ref[a], vn.astype(BF16)) + bs_ref[a]
            y_ref[rows, c0:c0 + A_HEAD_DIM] = (u * mixed * _silu(z)).astype(y_ref.dtype)

    cos = cos_ref[...]
    sin = sin_ref[...]
    half = B_DK // 2
    q0 = 3 * A_WIDTH
    k0 = q0 + B_HEADS * B_DK
    v0 = k0 + B_HEADS * B_DK
    z0 = v0 + B_WIDTH
    g_ret = retg_ref[...]

    def rot(xh):
        x1 = xh[:, :half]
        x2 = xh[:, half:]
        return jnp.concatenate([x1 * cos - x2 * sin, x1 * sin + x2 * cos], axis=-1)

    for h in range(B_HEADS):
        hc = h * B_DK
        q = rot(p_ref[:, q0 + hc:q0 + hc + B_DK].astype(F32))
        k = rot(p_ref[:, k0 + hc:k0 + hc + B_DK].astype(F32)) * (B_DK ** -0.5)
        v = p_ref[:, v0 + hc:v0 + hc + B_DV]
        z = p_ref[:, z0 + hc:z0 + hc + B_DV].astype(F32)
        s_old = s_scr[h]
        scores = _dot_nt(q.astype(BF16), k.astype(BF16)) * dec_scr[h]
        o = _dot(scores.astype(BF16), v) + _dot((q * qd_scr[h]).astype(BF16), s_old.astype(BF16))
        s_scr[h] = math.exp(ts * log_g[h]) * s_old + _dot_tn((k * kd_scr[h]).astype(BF16), v)
        yb = o * lax.rsqrt(jnp.mean(o * o, axis=-1, keepdims=True) + EPS) * g_ret
        y_ref[:, A_WIDTH + hc:A_WIDTH + hc + B_DV] = (yb * _silu(z)).astype(y_ref.dtype)

    @pl.when(t == nt - 1)
    def _():
        sfin_ref[0] = s_scr[...]


def _even_mix(p, cos, sin, wm, bs, ln_g, ret_g, s0, *, batch, length, ts, seg, emit_va):
    nt = length // ts
    has_s0 = s0 is not None
    in_specs = [
        pl.BlockSpec((ts, E_IN), lambda b, t: (b * nt + t, 0)),
        pl.BlockSpec((ts, B_DK // 2), lambda b, t: (t, 0)),
        pl.BlockSpec((ts, B_DK // 2), lambda b, t: (t, 0)),
        pl.BlockSpec((A_HEADS, seg, seg), lambda b, t: (0, 0, 0)),
        pl.BlockSpec((A_HEADS, seg, A_HEAD_DIM), lambda b, t: (0, 0, 0)),
        pl.BlockSpec((1, A_WIDTH), lambda b, t: (0, 0)),
        pl.BlockSpec((1, B_DV), lambda b, t: (0, 0)),
    ]
    args = [p, cos, sin, wm, bs, ln_g, ret_g]
    state_spec = pl.BlockSpec((1, B_HEADS, B_DK, B_DV), lambda b, t: (b, 0, 0, 0))
    if has_s0:
        in_specs.append(state_spec)
        args.append(s0)
    out_specs = [pl.BlockSpec((ts, E_MIX), lambda b, t: (b * nt + t, 0)), state_spec]
    out_shape = [jax.ShapeDtypeStruct((batch * length, E_MIX), BF16),
                 jax.ShapeDtypeStruct((batch, B_HEADS, B_DK, B_DV), F32)]
    if emit_va:
        out_specs.append(pl.BlockSpec((ts, A_WIDTH), lambda b, t: (b * nt + t, 0)))
        out_shape.append(jax.ShapeDtypeStruct((batch * length, A_WIDTH), F32))
    return pl.pallas_call(
        functools.partial(_even_mix_kernel, ts=ts, seg=seg, has_s0=has_s0, emit_va=emit_va),
        grid=(batch, nt),
        in_specs=in_specs,
        out_specs=out_specs,
        out_shape=out_shape,
        scratch_shapes=[
            pltpu.VMEM((B_HEADS, B_DK, B_DV), F32),
            pltpu.VMEM((B_HEADS, ts, ts), F32),
            pltpu.VMEM((B_HEADS, ts, B_DK), F32),
            pltpu.VMEM((B_HEADS, ts, B_DK), F32),
        ],
        compiler_params=_params("arbitrary", "arbitrary"),
        name="even_mix",
    )(*args)


def _odd_mix_kernel(p_ref, lr_ref, wgk_ref, bgk_ref, glag_ref, *rest, ts, blk, has_s0):
    rest = list(rest)
    s0_ref = rest.pop(0) if has_s0 else None
    y_ref, sfin_ref, st_scr = rest

    t = pl.program_id(1)
    nt = pl.num_programs(1)

    @pl.when(t == 0)
    def _():
        if has_s0:
            for h in range(C_HEADS):
                st_scr[h] = s0_ref[0, h].T
        else:
            st_scr[...] = jnp.zeros_like(st_scr)

    gk = _log_sigmoid(_dot(lr_ref[...].astype(BF16), wgk_ref[...]) + bgk_ref[...]) * (1.0 / GLA_GATE_NORM)

    row = lax.broadcasted_iota(jnp.int32, (blk, blk), 0)
    col = lax.broadcasted_iota(jnp.int32, (blk, blk), 1)
    causal = row >= col
    tril = jnp.where(causal, 1.0, 0.0).astype(BF16)
    g_gla = glag_ref[...]
    k0 = C_HEADS * C_DK
    v0 = 2 * C_HEADS * C_DK
    z0 = v0 + C_WIDTH

    for h in range(C_HEADS):
        kc = h * C_DK
        vc = h * C_DV
        for c in range(ts // blk):
            rows = slice(c * blk, (c + 1) * blk)
            g = gk[rows, kc:kc + C_DK]
            g_hi = g.astype(BF16)
            g_lo = (g - g_hi.astype(F32)).astype(BF16)
            bcum = _dot(tril, g_hi) + _dot(tril, g_lo)
            bend = bcum[blk - 1:blk, :]
            q = p_ref[rows, kc:kc + C_DK].astype(F32) * (C_DK ** -0.5)
            k = p_ref[rows, k0 + kc:k0 + kc + C_DK].astype(F32)
            v = p_ref[rows, v0 + vc:v0 + vc + C_DV]
            z = p_ref[rows, z0 + vc:z0 + vc + C_DV].astype(F32)
            qf = (q * jnp.exp(bcum)).astype(BF16)
            kf = (k * jnp.exp(-bcum)).astype(BF16)
            scores = jnp.where(causal, _dot_nt(qf, kf), 0.0).astype(BF16)
            st_old = st_scr[h]
            o = _dot(scores, v) + _dot_nt(qf, st_old.astype(BF16))
            k_tail = (k * jnp.exp(bend - bcum)).astype(BF16)
            st_scr[h] = st_old * jnp.exp(bend) + _dot_tn(v, k_tail)
            y = o * lax.rsqrt(jnp.mean(o * o, axis=-1, keepdims=True) + EPS) * g_gla
            y_ref[rows, vc:vc + C_DV] = (y * _silu(z)).astype(y_ref.dtype)

    @pl.when(t == nt - 1)
    def _():
        for h in range(C_HEADS):
            sfin_ref[0, h] = st_scr[h].T


def _odd_mix(p, lr, wgk, bgk, gla_g, s0, *, batch, length, ts, blk):
    nt = length // ts
    has_s0 = s0 is not None
    in_specs = [
        pl.BlockSpec((ts, O_MAIN), lambda b, t: (b * nt + t, 0)),
        pl.BlockSpec((ts, LR_PAD), lambda b, t: (b * nt + t, 0)),
        pl.BlockSpec((LR_PAD, C_HEADS * C_DK), lambda b, t: (0, 0)),
        pl.BlockSpec((1, C_HEADS * C_DK), lambda b, t: (0, 0)),
        pl.BlockSpec((1, C_DV), lambda b, t: (0, 0)),
    ]
    args = [p, lr, wgk, bgk, gla_g]
    state_spec = pl.BlockSpec((1, C_HEADS, C_DK, C_DV), lambda b, t: (b, 0, 0, 0))
    if has_s0:
        in_specs.append(state_spec)
        args.append(s0)
    return pl.pallas_call(
        functools.partial(_odd_mix_kernel, ts=ts, blk=blk, has_s0=has_s0),
        grid=(batch, nt),
        in_specs=in_specs,
        out_specs=[pl.BlockSpec((ts, C_WIDTH), lambda b, t: (b * nt + t, 0)), state_spec],
        out_shape=[jax.ShapeDtypeStruct((batch * length, C_WIDTH), BF16),
                   jax.ShapeDtypeStruct((batch, C_HEADS, C_DK, C_DV), F32)],
        scratch_shapes=[pltpu.VMEM((C_HEADS, C_DV, C_DK), F32)],
        compiler_params=_params("arbitrary", "arbitrary"),
        name="odd_mix",
    )(*args)


def _rope_tables(pos):
    half = B_DK // 2
    freqs = ROPE_BASE ** (-jnp.arange(half, dtype=F32) / half)
    ang = pos.astype(F32)[:, None] * freqs[None, :]
    return jnp.cos(ang), jnp.sin(ang)


def _trunk(x, mod, rows_per_group, pos, ret_state, gla_state, weights, *, batch, length, tm, ts, seg, gla_blk,
           emit_va):
    (norm_g, w_in_e, gmlp_ln_g, wm, bs, ret_norm_g, w_out_e, w_in_o, w_lr, w_gk, b_gk, gla_norm_g, w_out_o,
     final_g) = weights
    cos, sin = _rope_tables(pos)
    tiles_per_group = max(1, (length if rows_per_group == 1 else batch * length) // tm)
    ret_new, gla_new, v_rows = [], [], []
    for l in range(DEPTH):
        shift = mod[l, :, :, 0:D_MODEL]
        scale = mod[l, :, :, D_MODEL:2 * D_MODEL]
        gate = mod[l, :, :, 2 * D_MODEL:]
        i = l // 2
        g = norm_g[l].reshape(1, D_MODEL)
        if l % 2 == 0:
            (p,) = _in_proj(x, scale, shift, g, w_in_e[i], None, tm=tm, tiles_per_group=tiles_per_group)
            outs = _even_mix(p, cos, sin, wm[i], bs[i], gmlp_ln_g[i].reshape(1, A_WIDTH),
                             ret_norm_g[i].reshape(1, B_DV), None if ret_state is None else ret_state[i],
                             batch=batch, length=length, ts=ts, seg=seg, emit_va=emit_va)
            y, s_fin = outs[0], outs[1]
            ret_new.append(s_fin)
            if emit_va:
                v_rows.append(outs[2])
            x = _out_proj(y, w_out_e[i], x, gate, tm=tm, tiles_per_group=tiles_per_group)
        else:
            p, lr = _in_proj(x, scale, shift, g, w_in_o[i], w_lr[i], tm=tm, tiles_per_group=tiles_per_group)
            y, s_fin = _odd_mix(p, lr, w_gk[i], b_gk[i].reshape(1, C_HEADS * C_DK),
                                gla_norm_g[i].reshape(1, C_DV), None if gla_state is None else gla_state[i],
                                batch=batch, length=length, ts=ts, blk=gla_blk)
            gla_new.append(s_fin)
            x = _out_proj(y, w_out_o[i], x, gate, tm=tm, tiles_per_group=tiles_per_group)
    y_final = _final_norm(x, final_g.reshape(1, D_MODEL), tm=min(tm, 512))
    return y_final, jnp.stack(ret_new), jnp.stack(gla_new), v_rows


def _gmlp_weights(gmlp_ws, gmlp_bs, seg):
    p = np.arange(seg)
    mask = (p[None, :] // CHUNK) <= (p[:, None] // CHUNK)
    wm = jnp.where(mask[None, None], gmlp_ws[:, :, :seg, :seg], 0.0).astype(BF16)
    bs = jnp.broadcast_to(gmlp_bs[:, :, :seg, None], gmlp_bs.shape[:2] + (seg, A_HEAD_DIM))
    return wm, bs


def kernel(x_prompt, x_sample, state_ret, state_gla, c_prompt, c_sample, norm_g, w_mod, b_mod, w_in_e, gmlp_ln_g,
           gmlp_ws, gmlp_bs, ret_norm_g, w_out_e, w_in_o, w_gk2, b_gk, gla_norm_g, w_out_o, final_g):
    bp, lp, d = x_prompt.shape
    bs_, ls = x_sample.shape[:2]
    n_odd = w_in_o.shape[0]

    w_in_e_b = w_in_e.astype(BF16)
    w_out_e_b = w_out_e.astype(BF16)
    w_in_o_b = w_in_o[:, :, :O_MAIN].astype(BF16)
    w_lr = jnp.pad(w_in_o[:, :, O_MAIN:], ((0, 0), (0, 0), (0, LR_PAD - GLA_RANK))).astype(BF16)
    w_gk = jnp.pad(w_gk2, ((0, 0), (0, LR_PAD - GLA_RANK), (0, 0))).astype(BF16)
    w_out_o_b = w_out_o.astype(BF16)

    c_all = jnp.concatenate([c_prompt, c_sample], axis=0)
    mod = _modulation(c_all, w_mod, b_mod)
    mod_p = mod[:, :bp].reshape(DEPTH, bp, 1, 3 * d)
    mod_s = jnp.repeat(mod[:, bp:], ls, axis=1).reshape(DEPTH, 1, bs_ * ls, 3 * d)

    def weights_for(seg):
        wm, bs = _gmlp_weights(gmlp_ws, gmlp_bs, seg)
        return (norm_g, w_in_e_b, gmlp_ln_g, wm, bs, ret_norm_g, w_out_e_b, w_in_o_b, w_lr, w_gk, b_gk,
                gla_norm_g, w_out_o_b, final_g)

    y_p, ret_p, gla_p, _ = _trunk(
        x_prompt.reshape(bp * lp, d), mod_p, 1, jnp.arange(lp), None, None, weights_for(GMLP_CHUNK),
        batch=bp, length=lp, tm=1024, ts=256, seg=GMLP_CHUNK, gla_blk=64, emit_va=False)
    past_len = 1024
    y_s, ret_s, gla_s, v_rows = _trunk(
        x_sample.reshape(bs_ * ls, d), mod_s, bs_ * ls, past_len + jnp.arange(ls), state_ret, state_gla,
        weights_for(ls), batch=bs_, length=ls, tm=bs_ * ls, ts=ls, seg=ls, gla_blk=ls, emit_va=True)
    gmlp_v = jnp.stack(v_rows).reshape(len(v_rows), bs_, ls, A_WIDTH)
    return (y_p.reshape(bp, lp, d), y_s.reshape(bs_, ls, d), ret_p, ret_s, gla_p, gla_s, gmlp_v)
```

```python
import functools
import math

import jax
import jax.numpy as jnp
import numpy as np
from jax import lax
from jax.experimental import pallas as pl
from jax.experimental.pallas import tpu as pltpu

F32 = jnp.float32
BF16 = jnp.bfloat16

D_MODEL = 2048
DEPTH = 4
PAST_LEN = 1024
EPS = 1e-6
CHUNK = 64
A_HEADS = 4
A_HEAD_DIM = 512
A_WIDTH = 2048
GMLP_CHUNK = 128
B_HEADS = 8
B_DK = 256
B_DV = 256
B_WIDTH = 2048
ROPE_BASE = 10000.0
C_HEADS = 8
C_DK = 256
C_DV = 512
C_WIDTH = 4096
GLA_RANK = 16
GLA_GATE_NORM = 16.0
E_IN = 3 * A_WIDTH + 2 * B_HEADS * B_DK + 2 * B_WIDTH
E_MIX = A_WIDTH + B_WIDTH
O_MAIN = 2 * C_HEADS * C_DK + 2 * C_WIDTH
LR_PAD = 128

VMEM_LIMIT_BYTES = 56 * 1024 * 1024

PROMPT_TM = 1024
PROMPT_RC = 256
PROMPT_GLA_BLK = 64
PROJ_TN = 1024
OUT_TN = 512


def _params(*sem):
    return pltpu.CompilerParams(dimension_semantics=sem, vmem_limit_bytes=VMEM_LIMIT_BYTES)


def _dot(a, b):
    return jnp.dot(a, b, preferred_element_type=F32)


def _dot_nt(a, b):
    return lax.dot_general(a, b, (((1,), (1,)), ((), ())), preferred_element_type=F32)


def _dot_tn(a, b):
    return lax.dot_general(a, b, (((0,), (0,)), ((), ())), preferred_element_type=F32)


def _silu(x):
    return x * jax.nn.sigmoid(x)


def _log_sigmoid(x):
    return jnp.minimum(x, 0.0) - jnp.log1p(jnp.exp(-jnp.abs(x)))


def _rms(x, g):
    return x * lax.rsqrt(jnp.mean(x * x, axis=-1, keepdims=True) + EPS) * g


def _norm_modulate(x, g, scale, shift):
    return (_rms(x, g) * (1.0 + scale) + shift).astype(BF16)


def _retention_log_gamma():
    return [math.log1p(-(2.0 ** (-5.0 - h))) for h in range(B_HEADS)]


def _fill_retention_tables(dec_scr, qd_scr, kd_scr, n):
    log_g = _retention_log_gamma()
    row = lax.broadcasted_iota(jnp.int32, (n, n), 0).astype(F32)
    col = lax.broadcasted_iota(jnp.int32, (n, n), 1).astype(F32)
    diff = row - col
    rowd = lax.broadcasted_iota(jnp.int32, (n, B_DK // 2), 0).astype(F32)
    for h in range(B_HEADS):
        dec_scr[h] = jnp.where(diff >= 0, jnp.exp(jnp.maximum(diff, 0.0) * log_g[h]), 0.0)
        qd_scr[h] = jnp.exp((rowd + 1.0) * log_g[h])
        kd_scr[h] = jnp.exp((n - 1.0 - rowd) * log_g[h])


def _rotary(xh, cos, sin):
    half = xh.shape[-1] // 2
    x1 = xh[:, :half]
    x2 = xh[:, half:]
    return jnp.concatenate([x1 * cos - x2 * sin, x1 * sin + x2 * cos], axis=-1)


def _run_next(fillers):
    if fillers:
        fillers.pop(0)()


def _gmlp_unit(u, v, z, g_ln, wm, bias, seg, fillers=()):
    fillers = list(fillers)
    v = jax.nn.gelu(v)
    vc = v - jnp.mean(v, axis=-1, keepdims=True)
    vn = vc * lax.rsqrt(jnp.mean(vc * vc, axis=-1, keepdims=True) + EPS) * g_ln
    vb = vn.astype(BF16)
    _run_next(fillers)
    mixed = [_dot(wm, vb[s * seg:(s + 1) * seg]) + bias for s in range(u.shape[0] // seg)]
    mixed = mixed[0] if len(mixed) == 1 else jnp.concatenate(mixed, axis=0)
    _run_next(fillers)
    y = jax.nn.gelu(u) * mixed * _silu(z)
    while fillers:
        _run_next(fillers)
    return y, vn


def _retention_chunk(q, k, v, z, s_old, cos, sin, dec, qd, kd, g_ret, fillers=()):
    fillers = list(fillers)
    n = q.shape[0]
    qd = jnp.concatenate([qd, qd], axis=-1)
    kd = jnp.concatenate([kd, kd], axis=-1)
    q = _rotary(q, cos, sin)
    k = _rotary(k, cos, sin) * (B_DK ** -0.5)
    vb = v.astype(BF16)
    _run_next(fillers)
    scores = _dot_nt(q.astype(BF16), k.astype(BF16)) * dec
    s_new = qd[n - 1:n, :] * s_old + _dot_tn((k * kd).astype(BF16), vb)
    _run_next(fillers)
    o = _dot(scores.astype(BF16), vb) + _dot((q * qd).astype(BF16), s_old.astype(BF16))
    _run_next(fillers)
    y = _rms(o, g_ret) * _silu(z)
    while fillers:
        _run_next(fillers)
    return y, s_new


def _gla_chunk(q, k, v, z, bcum, st_old, causal, g_gla):
    n = q.shape[0]
    bend = bcum[n - 1:n, :]
    qf = (q * jnp.exp(bcum)).astype(BF16)
    kf = (k * jnp.exp(-bcum)).astype(BF16)
    scores = jnp.where(causal, _dot_nt(qf, kf), 0.0).astype(BF16)
    o = _dot(scores, v) + _dot_nt(qf, st_old.astype(BF16))
    k_tail = (k * jnp.exp(bend - bcum)).astype(BF16)
    st_new = st_old * jnp.exp(bend) + _dot_tn(v, k_tail)
    return _rms(o, g_gla) * _silu(z), st_new


def _gla_chunk_anchored(q, k, v, z, gk, tril, st_old, g_gla, sub, fillers):
    fillers = list(fillers)
    n = q.shape[0]
    bcum = _cumsum_rows(gk, tril)
    _run_next(fillers)
    bend = bcum[n - 1:n, :]
    scores = []
    for j in range(n // sub):
        lo, hi = j * sub, (j + 1) * sub
        width = -(-hi // 128) * 128
        anchor = bcum[lo - 1:lo, :] if j else jnp.zeros_like(bend)
        qj = (q[lo:hi] * jnp.exp(bcum[lo:hi] - anchor)).astype(BF16)
        row = lax.broadcasted_iota(jnp.int32, (width, C_DK), 0)
        kj = (k[:width] * jnp.exp(jnp.where(row < hi, anchor - bcum[:width], 0.0))).astype(BF16)
        t_idx = lax.broadcasted_iota(jnp.int32, (sub, width), 0) + lo
        s_idx = lax.broadcasted_iota(jnp.int32, (sub, width), 1)
        scores.append(jnp.where(t_idx >= s_idx, _dot_nt(qj, kj), 0.0).astype(BF16))
    q_state = (q * jnp.exp(bcum)).astype(BF16)
    k_tail = (k * jnp.exp(bend - bcum)).astype(BF16)
    _run_next(fillers)
    o = jnp.concatenate([_dot(s, v[:s.shape[1]]) for s in scores], axis=0) + _dot_nt(q_state, st_old.astype(BF16))
    st_new = st_old * jnp.exp(bend) + _dot_tn(v, k_tail)
    _run_next(fillers)
    y = _rms(o, g_gla) * _silu(z)
    while fillers:
        _run_next(fillers)
    return y, st_new


def _cumsum_rows(g, tril):
    g_hi = g.astype(BF16)
    g_lo = (g - g_hi.astype(F32)).astype(BF16)
    return _dot(tril, g_hi) + _dot(tril, g_lo)


def _block_tril(n, blk):
    row = lax.broadcasted_iota(jnp.int32, (n, n), 0)
    col = lax.broadcasted_iota(jnp.int32, (n, n), 1)
    return jnp.where((row >= col) & (row // blk == col // blk), 1.0, 0.0).astype(BF16)


def _causal(n):
    return lax.broadcasted_iota(jnp.int32, (n, n), 0) >= lax.broadcasted_iota(jnp.int32, (n, n), 1)


def _alias_prev(in_specs, args, prev, out_index):
    if prev is None:
        return {}
    in_specs.append(pl.BlockSpec(memory_space=pl.ANY))
    args.append(prev)
    return {len(args) - 1: out_index}


def _mod_kernel(c_ref, w_ref, b_ref, o_ref):
    s = _silu(c_ref[...]).astype(BF16)
    o_ref[0] = _dot(s, w_ref[0].astype(BF16)) + b_ref[0]


def _modulation(c_all, w_mod, b_mod):
    rows = c_all.shape[0]
    tn = 1024
    n = 3 * D_MODEL
    return pl.pallas_call(
        _mod_kernel,
        grid=(DEPTH, n // tn),
        in_specs=[
            pl.BlockSpec((rows, D_MODEL), lambda l, j: (0, 0)),
            pl.BlockSpec((1, D_MODEL, tn), lambda l, j: (l, 0, j)),
            pl.BlockSpec((1, 1, tn), lambda l, j: (l, 0, j)),
        ],
        out_specs=pl.BlockSpec((1, rows, tn), lambda l, j: (l, 0, j)),
        out_shape=jax.ShapeDtypeStruct((DEPTH, rows, n), F32),
        compiler_params=_params("arbitrary", "arbitrary"),
        name="modulation",
    )(c_all, w_mod, b_mod.reshape(DEPTH, 1, n))


def _fused_even_kernel(x_ref, sh_ref, sc_ref, g_ref, wu_ref, wva_ref, wza_ref, wq_ref, wk_ref, wv_ref, wz_ref,
                       cos_ref, sin_ref, wm_ref, bs_ref, lng_ref, retg_ref, *rest, tm, rc, aliased):
    rest = rest[1:] if aliased else rest
    ya_ref, yb_ref, sfin_ref, h_scr, s_scr, dec_scr, qd_scr, kd_scr = rest
    b = pl.program_id(0)
    j = pl.program_id(1)
    i = pl.program_id(2)

    @pl.when(j == 0)
    def _():
        h_scr[i] = _norm_modulate(x_ref[...], g_ref[0], sc_ref[...], sh_ref[...])

    @pl.when((b == 0) & (j == 0) & (i == 0))
    def _():
        _fill_retention_tables(dec_scr, qd_scr, kd_scr, rc)

    n_chunks = tm // rc

    def projection(c, w_refs):
        rows = slice(c * rc, (c + 1) * rc)
        p = [None] * len(w_refs)

        def piece(n):
            def run():
                p[n] = _dot(h_scr[i, rows, :], w_refs[n][0])
            return run

        return p, [piece(n) for n in range(len(w_refs))]

    @pl.when(j < A_HEADS)
    def _():
        w_refs = (wva_ref, wu_ref, wza_ref)
        cur, pieces = projection(0, w_refs)
        while pieces:
            _run_next(pieces)
        for c in range(n_chunks):
            nxt, pieces = projection(c + 1, w_refs) if c + 1 < n_chunks else (None, [])
            y, _ = _gmlp_unit(cur[1], cur[0], cur[2], lng_ref[0], wm_ref[0, 0], bs_ref[0, 0], GMLP_CHUNK, pieces)
            ya_ref[c * rc:(c + 1) * rc, :] = y.astype(ya_ref.dtype)
            cur = nxt

    @pl.when((j >= A_HEADS) & (i == 0))
    def _():
        s_scr[...] = jnp.zeros_like(s_scr)

    @pl.when(j >= A_HEADS)
    def _():
        hd = j - A_HEADS
        s = s_scr[...]
        dec = dec_scr[hd]
        qd = qd_scr[hd]
        kd = kd_scr[hd]
        w_refs = (wq_ref, wk_ref, wv_ref, wz_ref)
        cur, pieces = projection(0, w_refs)
        while pieces:
            _run_next(pieces)
        for c in range(n_chunks):
            rows = slice(c * rc, (c + 1) * rc)
            nxt, pieces = projection(c + 1, w_refs) if c + 1 < n_chunks else (None, [])
            y, s = _retention_chunk(cur[0], cur[1], cur[2], cur[3], s, cos_ref[rows, :], sin_ref[rows, :],
                                    dec, qd, kd, retg_ref[0], pieces)
            yb_ref[rows, :] = y.astype(yb_ref.dtype)
            cur = nxt
        s_scr[...] = s
        sfin_ref[0, 0, 0] = s


def _fused_even(x, mod5, mod_row0, norm_g3, w_in, cos, sin, wm, bs, ln_g3, ret_g3, prev_state, *, layer, batch, length):
    li = layer // 2
    tm, rc = PROMPT_TM, PROMPT_RC
    tpb = length // tm
    t = batch * length
    n_items = A_HEADS + B_HEADS

    def a_of(j):
        return jnp.minimum(j, A_HEADS - 1)

    def h_of(j):
        return jnp.maximum(j - A_HEADS, 0)

    def x_row(b, j, i):
        return b * tpb + jnp.where(j == 0, i, tpb - 1)

    def ya_row(b, j, i):
        return b * tpb + jnp.where(j < A_HEADS, i, tpb - 1)

    def yb_row(b, j, i):
        return b * tpb + jnp.where(j >= A_HEADS, i, 0)

    def mod_spec(which):
        return pl.BlockSpec((None, None, None, 1, D_MODEL), lambda b, j, i: (layer, mod_row0 + b, which, 0, 0))

    def wa_spec(seg):
        return pl.BlockSpec((1, D_MODEL, A_HEAD_DIM), lambda b, j, i: (li, 0, seg * A_HEADS + a_of(j)))

    def wb_spec(seg):
        base = 3 * A_WIDTH // B_DK + seg * B_HEADS
        return pl.BlockSpec((1, D_MODEL, B_DK), lambda b, j, i: (li, 0, base + h_of(j)))

    in_specs = [
        pl.BlockSpec((tm, D_MODEL), lambda b, j, i: (x_row(b, j, i), 0), pipeline_mode=pl.Buffered(1)),
        mod_spec(0), mod_spec(1),
        pl.BlockSpec((1, 1, D_MODEL), lambda b, j, i: (layer, 0, 0)),
        wa_spec(0), wa_spec(1), wa_spec(2),
        wb_spec(0), wb_spec(1), wb_spec(2), wb_spec(3),
        pl.BlockSpec((tm, B_DK // 2), lambda b, j, i: (i, 0)),
        pl.BlockSpec((tm, B_DK // 2), lambda b, j, i: (i, 0)),
        pl.BlockSpec((1, 1, GMLP_CHUNK, GMLP_CHUNK), lambda b, j, i: (li, a_of(j), 0, 0)),
        pl.BlockSpec((1, 1, GMLP_CHUNK, A_HEAD_DIM), lambda b, j, i: (li, a_of(j), 0, 0)),
        pl.BlockSpec((1, 1, A_HEAD_DIM), lambda b, j, i: (li, 0, a_of(j))),
        pl.BlockSpec((1, 1, B_DV), lambda b, j, i: (li, 0, 0)),
    ]
    args = [x, mod5, mod5, norm_g3, w_in, w_in, w_in, w_in, w_in, w_in, w_in, cos, sin, wm, bs, ln_g3, ret_g3]
    aliases = _alias_prev(in_specs, args, prev_state, 2)
    n_state = DEPTH // 2
    return pl.pallas_call(
        functools.partial(_fused_even_kernel, tm=tm, rc=rc, aliased=prev_state is not None),
        grid=(batch, n_items, tpb),
        in_specs=in_specs,
        out_specs=[
            pl.BlockSpec((tm, A_HEAD_DIM), lambda b, j, i: (ya_row(b, j, i), a_of(j))),
            pl.BlockSpec((tm, B_DV), lambda b, j, i: (yb_row(b, j, i), h_of(j))),
            pl.BlockSpec((1, 1, 1, B_DK, B_DV), lambda b, j, i: (li, b, h_of(j), 0, 0)),
        ],
        out_shape=[
            jax.ShapeDtypeStruct((t, A_WIDTH), BF16),
            jax.ShapeDtypeStruct((t, B_WIDTH), BF16),
            jax.ShapeDtypeStruct((n_state, batch, B_HEADS, B_DK, B_DV), F32),
        ],
        scratch_shapes=[
            pltpu.VMEM((tpb, tm, D_MODEL), BF16),
            pltpu.VMEM((B_DK, B_DV), F32),
            pltpu.VMEM((B_HEADS, rc, rc), F32),
            pltpu.VMEM((B_HEADS, rc, B_DK // 2), F32),
            pltpu.VMEM((B_HEADS, rc, B_DK // 2), F32),
        ],
        input_output_aliases=aliases,
        compiler_params=_params("arbitrary", "arbitrary", "arbitrary"),
        name="fused_even",
    )(*args)


def _fused_odd_kernel(x_ref, sh_ref, sc_ref, g_ref, wq_ref, wk_ref, wv_ref, wz_ref, wlr_ref, wgk_ref, bgk_ref,
                      glag_ref, *rest, tm, rc, blk, aliased):
    rest = rest[1:] if aliased else rest
    y_ref, sfin_ref, h_scr, lr_scr, st_scr = rest
    hd = pl.program_id(1)
    i = pl.program_id(2)

    @pl.when(hd == 0)
    def _():
        h = _norm_modulate(x_ref[...], g_ref[0], sc_ref[...], sh_ref[...])
        h_scr[i] = h
        lr_scr[i] = _dot(h, wlr_ref[0]).astype(BF16)

    @pl.when(i == 0)
    def _():
        st_scr[...] = jnp.zeros_like(st_scr)

    def projection(c):
        rows = slice(c * rc, (c + 1) * rc)
        p = {}

        def piece_q():
            p["q"] = _dot(h_scr[i, rows, :], wq_ref[0]) * (C_DK ** -0.5)
            p["gk"] = _log_sigmoid(_dot(lr_scr[i, rows, :], wgk_ref[0]) + bgk_ref[0]) * (1.0 / GLA_GATE_NORM)

        def piece_k():
            p["k"] = _dot(h_scr[i, rows, :], wk_ref[0])

        def piece_v():
            p["v"] = _dot(h_scr[i, rows, :], wv_ref[0]).astype(BF16)

        def piece_z():
            p["z"] = _dot(h_scr[i, rows, :], wz_ref[0])

        return p, [piece_q, piece_k, piece_v, piece_z]

    tril = _block_tril(rc, rc)
    st = st_scr[...]
    n_chunks = tm // rc
    cur, pieces = projection(0)
    while pieces:
        _run_next(pieces)
    for c in range(n_chunks):
        nxt, pieces = projection(c + 1) if c + 1 < n_chunks else (None, [])
        y, st = _gla_chunk_anchored(cur["q"], cur["k"], cur["v"], cur["z"], cur["gk"], tril, st, glag_ref[0], blk,
                                    pieces)
        y_ref[c * rc:(c + 1) * rc, :] = y.astype(y_ref.dtype)
        cur = nxt
    st_scr[...] = st
    sfin_ref[0, 0, 0] = st.T


def _fused_odd(x, mod5, mod_row0, norm_g3, w_in, w_lr, w_gk, b_gk3, gla_g3, prev_state, *, layer, batch, length):
    li = layer // 2
    tm, rc, blk = PROMPT_TM, PROMPT_RC, PROMPT_GLA_BLK
    tpb = length // tm
    t = batch * length
    v_base = 2 * C_HEADS * C_DK // C_DV

    def x_row(b, h, i):
        return b * tpb + jnp.where(h == 0, i, tpb - 1)

    def mod_spec(which):
        return pl.BlockSpec((None, None, None, 1, D_MODEL), lambda b, h, i: (layer, mod_row0 + b, which, 0, 0))

    in_specs = [
        pl.BlockSpec((tm, D_MODEL), lambda b, h, i: (x_row(b, h, i), 0), pipeline_mode=pl.Buffered(1)),
        mod_spec(0), mod_spec(1),
        pl.BlockSpec((1, 1, D_MODEL), lambda b, h, i: (layer, 0, 0)),
        pl.BlockSpec((1, D_MODEL, C_DK), lambda b, h, i: (li, 0, h)),
        pl.BlockSpec((1, D_MODEL, C_DK), lambda b, h, i: (li, 0, C_HEADS + h)),
        pl.BlockSpec((1, D_MODEL, C_DV), lambda b, h, i: (li, 0, v_base + h)),
        pl.BlockSpec((1, D_MODEL, C_DV), lambda b, h, i: (li, 0, v_base + C_HEADS + h)),
        pl.BlockSpec((1, D_MODEL, LR_PAD), lambda b, h, i: (li, 0, 0)),
        pl.BlockSpec((1, LR_PAD, C_DK), lambda b, h, i: (li, 0, h)),
        pl.BlockSpec((1, 1, C_DK), lambda b, h, i: (li, 0, h)),
        pl.BlockSpec((1, 1, C_DV), lambda b, h, i: (li, 0, 0)),
    ]
    args = [x, mod5, mod5, norm_g3, w_in, w_in, w_in, w_in, w_lr, w_gk, b_gk3, gla_g3]
    aliases = _alias_prev(in_specs, args, prev_state, 1)
    n_state = DEPTH // 2
    return pl.pallas_call(
        functools.partial(_fused_odd_kernel, tm=tm, rc=rc, blk=blk, aliased=prev_state is not None),
        grid=(batch, C_HEADS, tpb),
        in_specs=in_specs,
        out_specs=[
            pl.BlockSpec((tm, C_DV), lambda b, h, i: (b * tpb + i, h)),
            pl.BlockSpec((1, 1, 1, C_DK, C_DV), lambda b, h, i: (li, b, h, 0, 0)),
        ],
        out_shape=[
            jax.ShapeDtypeStruct((t, C_WIDTH), BF16),
            jax.ShapeDtypeStruct((n_state, batch, C_HEADS, C_DK, C_DV), F32),
        ],
        scratch_shapes=[
            pltpu.VMEM((tpb, tm, D_MODEL), BF16),
            pltpu.VMEM((tpb, tm, LR_PAD), BF16),
            pltpu.VMEM((C_DV, C_DK), F32),
        ],
        input_output_aliases=aliases,
        compiler_params=_params("arbitrary", "arbitrary", "arbitrary"),
        name="fused_odd",
    )(*args)


def _out_proj_kernel(*refs, n_parts):
    y_refs = refs[:n_parts]
    w_refs = refs[n_parts:2 * n_parts]
    x_ref, gate_ref, o_ref = refs[2 * n_parts:]
    acc = _dot(y_refs[0][...], w_refs[0][0])
    for p in range(1, n_parts):
        acc = acc + _dot(y_refs[p][...], w_refs[p][0])
    o_ref[...] = x_ref[...] + gate_ref[...] * acc


def _out_proj(y_parts, w, x, gate_arr, gate_spec, *, layer, tm):
    li = layer // 2
    t, d = x.shape
    tn = OUT_TN
    n_parts = len(y_parts)
    kp = w.shape[1] // n_parts
    in_specs = [pl.BlockSpec((tm, kp), lambda i, j: (i, 0)) for _ in range(n_parts)]
    in_specs += [pl.BlockSpec((1, kp, tn), functools.partial(lambda i, j, p: (li, p, j), p=p)) for p in range(n_parts)]
    in_specs += [pl.BlockSpec((tm, tn), lambda i, j: (i, j)), gate_spec]
    return pl.pallas_call(
        functools.partial(_out_proj_kernel, n_parts=n_parts),
        grid=(t // tm, d // tn),
        in_specs=in_specs,
        out_specs=pl.BlockSpec((tm, tn), lambda i, j: (i, j)),
        out_shape=jax.ShapeDtypeStruct((t, d), F32),
        compiler_params=_params("arbitrary", "arbitrary"),
        name="out_proj",
    )(*y_parts, *([w] * n_parts), x, gate_arr)


def _final_norm_kernel(x_ref, g_ref, o_ref):
    o_ref[...] = _rms(x_ref[...], g_ref[...])


def _final_norm(x, g, *, tm):
    t, d = x.shape
    return pl.pallas_call(
        _final_norm_kernel,
        grid=(t // tm,),
        in_specs=[pl.BlockSpec((tm, d), lambda i: (i, 0)), pl.BlockSpec((1, d), lambda i: (0, 0))],
        out_specs=pl.BlockSpec((tm, d), lambda i: (i, 0)),
        out_shape=jax.ShapeDtypeStruct((t, d), F32),
        compiler_params=_params("arbitrary"),
        name="final_norm",
    )(x, g)


def _in_proj_kernel(x_ref, sh_ref, sc_ref, g_ref, w_ref, *rest, has_lr):
    if has_lr:
        wlr_ref, o_ref, lr_ref, h_scr = rest
    else:
        o_ref, h_scr = rest

    @pl.when(pl.program_id(0) == 0)
    def _():
        h = _norm_modulate(x_ref[...], g_ref[0], sc_ref[...], sh_ref[...])
        h_scr[...] = h
        if has_lr:
            lr_ref[...] = _dot(h, wlr_ref[0]).astype(BF16)

    o_ref[...] = _dot(h_scr[...], w_ref[0]).astype(o_ref.dtype)


def _in_proj(x, mod_rows, norm_g3, w, wlr, *, layer):
    li = layer // 2
    t, d = x.shape
    n = w.shape[2]
    tn = PROJ_TN
    has_lr = wlr is not None
    in_specs = [
        pl.BlockSpec((t, d), lambda j: (0, 0)),
        pl.BlockSpec((None, None, t, d), lambda j: (layer, 0, 0, 0)),
        pl.BlockSpec((None, None, t, d), lambda j: (layer, 1, 0, 0)),
        pl.BlockSpec((1, 1, d), lambda j: (layer, 0, 0)),
        pl.BlockSpec((1, d, tn), lambda j: (li, 0, j)),
    ]
    args = [x, mod_rows, mod_rows, norm_g3, w]
    out_specs = [pl.BlockSpec((t, tn), lambda j: (0, j))]
    out_shape = [jax.ShapeDtypeStruct((t, n), BF16)]
    if has_lr:
        in_specs.append(pl.BlockSpec((1, d, LR_PAD), lambda j: (li, 0, 0)))
        args.append(wlr)
        out_specs.append(pl.BlockSpec((t, LR_PAD), lambda j: (0, 0)))
        out_shape.append(jax.ShapeDtypeStruct((t, LR_PAD), BF16))
    return pl.pallas_call(
        functools.partial(_in_proj_kernel, has_lr=has_lr),
        grid=(n // tn,),
        in_specs=in_specs,
        out_specs=out_specs,
        out_shape=out_shape,
        scratch_shapes=[pltpu.VMEM((t, d), BF16)],
        compiler_params=_params("arbitrary"),
        name="in_proj",
    )(*args)


def _even_mix_kernel(p_ref, cos_ref, sin_ref, wm_ref, bs_ref, lng_ref, retg_ref, s0_ref, *rest, ts, aliased):
    rest = rest[2:] if aliased else rest
    y_ref, sfin_ref, va_ref, dec_scr, qd_scr, kd_scr = rest

    @pl.when(pl.program_id(0) == 0)
    def _():
        _fill_retention_tables(dec_scr, qd_scr, kd_scr, ts)

    for a in range(A_HEADS):
        c0 = a * A_HEAD_DIM
        u = p_ref[:, c0:c0 + A_HEAD_DIM].astype(F32)
        v = p_ref[:, A_WIDTH + c0:A_WIDTH + c0 + A_HEAD_DIM].astype(F32)
        z = p_ref[:, 2 * A_WIDTH + c0:2 * A_WIDTH + c0 + A_HEAD_DIM].astype(F32)
        y, vn = _gmlp_unit(u, v, z, lng_ref[0, :, c0:c0 + A_HEAD_DIM], wm_ref[0, a], bs_ref[0, a], ts)
        va_ref[0, :, c0:c0 + A_HEAD_DIM] = vn
        y_ref[:, c0:c0 + A_HEAD_DIM] = y.astype(y_ref.dtype)

    q0 = 3 * A_WIDTH
    k0 = q0 + B_HEADS * B_DK
    v0 = k0 + B_HEADS * B_DK
    z0 = v0 + B_WIDTH
    cos = cos_ref[...]
    sin = sin_ref[...]
    for h in range(B_HEADS):
        hc = h * B_DK
        y, s_new = _retention_chunk(
            p_ref[:, q0 + hc:q0 + hc + B_DK].astype(F32), p_ref[:, k0 + hc:k0 + hc + B_DK].astype(F32),
            p_ref[:, v0 + hc:v0 + hc + B_DV].astype(F32), p_ref[:, z0 + hc:z0 + hc + B_DV].astype(F32),
            s0_ref[0, 0, h], cos, sin, dec_scr[h], qd_scr[h], kd_scr[h], retg_ref[0])
        sfin_ref[0, 0, h] = s_new
        y_ref[:, A_WIDTH + hc:A_WIDTH + hc + B_DV] = y.astype(y_ref.dtype)


def _even_mix(p, cos, sin, wm, bs, ln_g3, ret_g3, state, prev_state, prev_va, *, layer, batch, ts):
    li = layer // 2
    n_state = DEPTH // 2
    in_specs = [
        pl.BlockSpec((ts, E_IN), lambda b: (b, 0)),
        pl.BlockSpec((ts, B_DK // 2), lambda b: (0, 0)),
        pl.BlockSpec((ts, B_DK // 2), lambda b: (0, 0)),
        pl.BlockSpec((1, A_HEADS, ts, ts), lambda b: (li, 0, 0, 0)),
        pl.BlockSpec((1, A_HEADS, ts, A_HEAD_DIM), lambda b: (li, 0, 0, 0)),
        pl.BlockSpec((1, 1, A_WIDTH), lambda b: (li, 0, 0)),
        pl.BlockSpec((1, 1, B_DV), lambda b: (li, 0, 0)),
        pl.BlockSpec((1, 1, B_HEADS, B_DK, B_DV), lambda b: (li, b, 0, 0, 0)),
    ]
    args = [p, cos, sin, wm, bs, ln_g3, ret_g3, state]
    aliases = _alias_prev(in_specs, args, prev_state, 1)
    aliases.update(_alias_prev(in_specs, args, prev_va, 2))
    return pl.pallas_call(
        functools.partial(_even_mix_kernel, ts=ts, aliased=prev_state is not None),
        grid=(batch,),
        in_specs=in_specs,
        out_specs=[
            pl.BlockSpec((ts, E_MIX), lambda b: (b, 0)),
            pl.BlockSpec((1, 1, B_HEADS, B_DK, B_DV), lambda b: (li, b, 0, 0, 0)),
            pl.BlockSpec((1, ts, A_WIDTH), lambda b: (li, b, 0)),
        ],
        out_shape=[
            jax.ShapeDtypeStruct((batch * ts, E_MIX), BF16),
            jax.ShapeDtypeStruct((n_state, batch, B_HEADS, B_DK, B_DV), F32),
            jax.ShapeDtypeStruct((n_state, batch * ts, A_WIDTH), F32),
        ],
        scratch_shapes=[
            pltpu.VMEM((B_HEADS, ts, ts), F32),
            pltpu.VMEM((B_HEADS, ts, B_DK // 2), F32),
            pltpu.VMEM((B_HEADS, ts, B_DK // 2), F32),
        ],
        input_output_aliases=aliases,
        compiler_params=_params("arbitrary"),
        name="even_mix",
    )(*args)


def _odd_mix_kernel(p_ref, lr_ref, wgk_ref, bgk_ref, glag_ref, s0_ref, *rest, ts, aliased):
    rest = rest[1:] if aliased else rest
    y_ref, sfin_ref = rest
    gk = _log_sigmoid(_dot(lr_ref[...], wgk_ref[0]) + bgk_ref[0]) * (1.0 / GLA_GATE_NORM)
    tril = _block_tril(ts, ts)
    causal = _causal(ts)
    k0 = C_HEADS * C_DK
    v0 = 2 * C_HEADS * C_DK
    z0 = v0 + C_WIDTH
    for h in range(C_HEADS):
        kc = h * C_DK
        vc = h * C_DV
        bcum = _cumsum_rows(gk[:, kc:kc + C_DK], tril)
        q = p_ref[:, kc:kc + C_DK].astype(F32) * (C_DK ** -0.5)
        k = p_ref[:, k0 + kc:k0 + kc + C_DK].astype(F32)
        v = p_ref[:, v0 + vc:v0 + vc + C_DV]
        z = p_ref[:, z0 + vc:z0 + vc + C_DV].astype(F32)
        y, st = _gla_chunk(q, k, v, z, bcum, s0_ref[0, 0, h].T, causal, glag_ref[0])
        sfin_ref[0, 0, h] = st.T
        y_ref[:, vc:vc + C_DV] = y.astype(y_ref.dtype)


def _odd_mix(p, lr, w_gk, b_gk3, gla_g3, state, prev_state, *, layer, batch, ts):
    li = layer // 2
    n_state = DEPTH // 2
    in_specs = [
        pl.BlockSpec((ts, O_MAIN), lambda b: (b, 0)),
        pl.BlockSpec((ts, LR_PAD), lambda b: (b, 0)),
        pl.BlockSpec((1, LR_PAD, C_HEADS * C_DK), lambda b: (li, 0, 0)),
        pl.BlockSpec((1, 1, C_HEADS * C_DK), lambda b: (li, 0, 0)),
        pl.BlockSpec((1, 1, C_DV), lambda b: (li, 0, 0)),
        pl.BlockSpec((1, 1, C_HEADS, C_DK, C_DV), lambda b: (li, b, 0, 0, 0)),
    ]
    args = [p, lr, w_gk, b_gk3, gla_g3, state]
    aliases = _alias_prev(in_specs, args, prev_state, 1)
    return pl.pallas_call(
        functools.partial(_odd_mix_kernel, ts=ts, aliased=prev_state is not None),
        grid=(batch,),
        in_specs=in_specs,
        out_specs=[
            pl.BlockSpec((ts, C_WIDTH), lambda b: (b, 0)),
            pl.BlockSpec((1, 1, C_HEADS, C_DK, C_DV), lambda b: (li, b, 0, 0, 0)),
        ],
        out_shape=[
            jax.ShapeDtypeStruct((batch * ts, C_WIDTH), BF16),
            jax.ShapeDtypeStruct((n_state, batch, C_HEADS, C_DK, C_DV), F32),
        ],
        input_output_aliases=aliases,
        compiler_params=_params("arbitrary"),
        name="odd_mix",
    )(*args)


def _rope_tables(pos):
    half = B_DK // 2
    freqs = ROPE_BASE ** (-jnp.arange(half, dtype=F32) / half)
    ang = pos.astype(F32)[:, None] * freqs[None, :]
    return jnp.cos(ang), jnp.sin(ang)


def _gmlp_weights(gmlp_ws, gmlp_bs, seg):
    p = np.arange(seg)
    mask = (p[None, :] // CHUNK) <= (p[:, None] // CHUNK)
    wm = jnp.where(mask[None, None], gmlp_ws[:, :, :seg, :seg], 0.0).astype(BF16)
    bs = jnp.broadcast_to(gmlp_bs[:, :, :seg, None], gmlp_bs.shape[:2] + (seg, A_HEAD_DIM))
    return wm, bs


def _prompt_trunk(x, mod5, w, *, batch, length):
    cos, sin = _rope_tables(jnp.arange(length))
    wm, bs = _gmlp_weights(w["gmlp_ws"], w["gmlp_bs"], GMLP_CHUNK)
    tpb = length // PROMPT_TM
    gate_spec = lambda layer: pl.BlockSpec((None, None, None, 1, OUT_TN),
                                           lambda i, j: (layer, i // tpb, 2, 0, j))
    ret, gla = None, None
    for layer in range(DEPTH):
        if layer % 2 == 0:
            ya, yb, ret = _fused_even(x, mod5, 0, w["norm_g"], w["w_in_e"], cos, sin, wm, bs, w["gmlp_ln_g"],
                                      w["ret_norm_g"], ret, layer=layer, batch=batch, length=length)
            x = _out_proj([ya, yb], w["w_out_e"], x, mod5, gate_spec(layer), layer=layer, tm=PROMPT_TM)
        else:
            y, gla = _fused_odd(x, mod5, 0, w["norm_g"], w["w_in_o"], w["w_lr"], w["w_gk"], w["b_gk"],
                                w["gla_norm_g"], gla, layer=layer, batch=batch, length=length)
            x = _out_proj([y], w["w_out_o"], x, mod5, gate_spec(layer), layer=layer, tm=PROMPT_TM)
    return _final_norm(x, w["final_g"], tm=512), ret, gla


def _sample_trunk(x, mod_rows, state_ret, state_gla, w, *, batch, length):
    t = batch * length
    cos, sin = _rope_tables(PAST_LEN + jnp.arange(length))
    wm, bs = _gmlp_weights(w["gmlp_ws"], w["gmlp_bs"], length)
    gate_spec = lambda layer: pl.BlockSpec((None, None, t, OUT_TN), lambda i, j: (layer, 2, 0, j))
    ret, gla, va = None, None, None
    for layer in range(DEPTH):
        if layer % 2 == 0:
            (p,) = _in_proj(x, mod_rows, w["norm_g"], w["w_in_e"], None, layer=layer)
            y, ret, va = _even_mix(p, cos, sin, wm, bs, w["gmlp_ln_g"], w["ret_norm_g"], state_ret, ret, va,
                                   layer=layer, batch=batch, ts=length)
            x = _out_proj([y], w["w_out_e"], x, mod_rows, gate_spec(layer), layer=layer, tm=t)
        else:
            p, lr = _in_proj(x, mod_rows, w["norm_g"], w["w_in_o"], w["w_lr"], layer=layer)
            y, gla = _odd_mix(p, lr, w["w_gk"], w["b_gk"], w["gla_norm_g"], state_gla, gla,
                              layer=layer, batch=batch, ts=length)
            x = _out_proj([y], w["w_out_o"], x, mod_rows, gate_spec(layer), layer=layer, tm=t)
    return _final_norm(x, w["final_g"], tm=t), ret, gla, va


def kernel(x_prompt, x_sample, state_ret, state_gla, c_prompt, c_sample, norm_g, w_mod, b_mod, w_in_e, gmlp_ln_g,
           gmlp_ws, gmlp_bs, ret_norm_g, w_out_e, w_in_o, w_gk2, b_gk, gla_norm_g, w_out_o, final_g):
    bp, lp, d = x_prompt.shape
    bs_, ls = x_sample.shape[:2]
    n_even, n_odd = w_in_e.shape[0], w_in_o.shape[0]

    w = {
        "norm_g": norm_g.reshape(DEPTH, 1, d),
        "w_in_e": w_in_e.astype(BF16),
        "w_out_e": w_out_e.astype(BF16),
        "w_in_o": w_in_o[:, :, :O_MAIN].astype(BF16),
        "w_lr": jnp.pad(w_in_o[:, :, O_MAIN:], ((0, 0), (0, 0), (0, LR_PAD - GLA_RANK))).astype(BF16),
        "w_gk": jnp.pad(w_gk2, ((0, 0), (0, LR_PAD - GLA_RANK), (0, 0))).astype(BF16),
        "w_out_o": w_out_o.astype(BF16),
        "gmlp_ws": gmlp_ws, "gmlp_bs": gmlp_bs,
        "gmlp_ln_g": gmlp_ln_g.reshape(n_even, 1, A_WIDTH),
        "ret_norm_g": ret_norm_g.reshape(n_even, 1, B_DV),
        "b_gk": b_gk.reshape(n_odd, 1, C_HEADS * C_DK),
        "gla_norm_g": gla_norm_g.reshape(n_odd, 1, C_DV),
        "final_g": final_g.reshape(1, d),
    }

    c_all = jnp.concatenate([c_prompt, c_sample], axis=0)
    mod = _modulation(c_all, w_mod, b_mod)
    mod5 = mod.reshape(DEPTH, bp + bs_, 3, 1, d)
    mod_rows = jnp.repeat(mod[:, bp:].reshape(DEPTH, bs_, 3, d).transpose(0, 2, 1, 3), ls, axis=2)

    y_p, ret_p, gla_p = _prompt_trunk(x_prompt.reshape(bp * lp, d), mod5, w, batch=bp, length=lp)
    y_s, ret_s, gla_s, va = _sample_trunk(x_sample.reshape(bs_ * ls, d), mod_rows, state_ret, state_gla, w,
                                          batch=bs_, length=ls)
    return (y_p.reshape(bp, lp, d), y_s.reshape(bs_, ls, d), ret_p, ret_s, gla_p, gla_s,
            va.reshape(n_even, bs_, ls, A_WIDTH))
```

```python
import functools
import math

import jax
import jax.numpy as jnp
import numpy as np
from jax import lax
from jax.experimental import pallas as pl
from jax.experimental.pallas import tpu as pltpu

F32 = jnp.float32
BF16 = jnp.bfloat16

D_MODEL = 2048
DEPTH = 4
PAST_LEN = 1024
EPS = 1e-6
CHUNK = 64
A_HEADS = 4
A_HEAD_DIM = 512
A_WIDTH = 2048
GMLP_CHUNK = 128
B_HEADS = 8
B_DK = 256
B_DV = 256
B_WIDTH = 2048
ROPE_BASE = 10000.0
C_HEADS = 8
C_DK = 256
C_DV = 512
C_WIDTH = 4096
GLA_RANK = 16
GLA_GATE_NORM = 16.0
E_IN = 3 * A_WIDTH + 2 * B_HEADS * B_DK + 2 * B_WIDTH
E_MIX = A_WIDTH + B_WIDTH
O_MAIN = 2 * C_HEADS * C_DK + 2 * C_WIDTH
LR_PAD = 128

VMEM_LIMIT_BYTES = 56 * 1024 * 1024

PROMPT_TM = 1024
PROMPT_RC = 256
PROMPT_GLA_BLK = 64
PROJ_TN = 1024
OUT_TN = 512
OUT_NORM_TM = 512


def _params(*sem):
    return pltpu.CompilerParams(dimension_semantics=sem, vmem_limit_bytes=VMEM_LIMIT_BYTES)


def _dot(a, b):
    return jnp.dot(a, b, preferred_element_type=F32)


def _dot_nt(a, b):
    return lax.dot_general(a, b, (((1,), (1,)), ((), ())), preferred_element_type=F32)


def _dot_tn(a, b):
    return lax.dot_general(a, b, (((0,), (0,)), ((), ())), preferred_element_type=F32)


def _silu(x):
    return x * jax.nn.sigmoid(x)


def _log_sigmoid(x):
    return jnp.minimum(x, 0.0) - jnp.log1p(jnp.exp(-jnp.abs(x)))


def _rms(x, g):
    return x * lax.rsqrt(jnp.mean(x * x, axis=-1, keepdims=True) + EPS) * g


def _norm_modulate(x, g, scale, shift):
    return (_rms(x, g) * (1.0 + scale) + shift).astype(BF16)


def _retention_log_gamma():
    return [math.log1p(-(2.0 ** (-5.0 - h))) for h in range(B_HEADS)]


def _fill_retention_tables(dec_scr, qd_scr, kd_scr, n):
    log_g = _retention_log_gamma()
    row = lax.broadcasted_iota(jnp.int32, (n, n), 0).astype(F32)
    col = lax.broadcasted_iota(jnp.int32, (n, n), 1).astype(F32)
    diff = row - col
    rowd = lax.broadcasted_iota(jnp.int32, (n, B_DK // 2), 0).astype(F32)
    for h in range(B_HEADS):
        dec_scr[h] = jnp.where(diff >= 0, jnp.exp(jnp.maximum(diff, 0.0) * log_g[h]), 0.0)
        qd_scr[h] = jnp.exp((rowd + 1.0) * log_g[h])
        kd_scr[h] = jnp.exp((n - 1.0 - rowd) * log_g[h])


def _rotary(xh, cos, sin):
    half = xh.shape[-1] // 2
    x1 = xh[:, :half]
    x2 = xh[:, half:]
    return jnp.concatenate([x1 * cos - x2 * sin, x1 * sin + x2 * cos], axis=-1)


def _run_next(fillers):
    if fillers:
        fillers.pop(0)()


def _gmlp_unit(u, v, z, g_ln, wm, bias, seg, fillers=()):
    fillers = list(fillers)
    v = jax.nn.gelu(v)
    vc = v - jnp.mean(v, axis=-1, keepdims=True)
    vn = vc * lax.rsqrt(jnp.mean(vc * vc, axis=-1, keepdims=True) + EPS) * g_ln
    vb = vn.astype(BF16)
    _run_next(fillers)
    mixed = [_dot(wm, vb[s * seg:(s + 1) * seg]) + bias for s in range(u.shape[0] // seg)]
    mixed = mixed[0] if len(mixed) == 1 else jnp.concatenate(mixed, axis=0)
    _run_next(fillers)
    y = jax.nn.gelu(u) * mixed * _silu(z)
    while fillers:
        _run_next(fillers)
    return y, vn


def _retention_chunk(q, k, v, z, s_old, cos, sin, dec, qd, kd, g_ret, fillers=()):
    fillers = list(fillers)
    n = q.shape[0]
    qd = jnp.concatenate([qd, qd], axis=-1)
    kd = jnp.concatenate([kd, kd], axis=-1)
    q = _rotary(q, cos, sin)
    k = _rotary(k, cos, sin) * (B_DK ** -0.5)
    vb = v.astype(BF16)
    _run_next(fillers)
    scores = _dot_nt(q.astype(BF16), k.astype(BF16)) * dec
    s_new = qd[n - 1:n, :] * s_old + _dot_tn((k * kd).astype(BF16), vb)
    _run_next(fillers)
    o = _dot(scores.astype(BF16), vb) + _dot((q * qd).astype(BF16), s_old.astype(BF16))
    _run_next(fillers)
    y = _rms(o, g_ret) * _silu(z)
    while fillers:
        _run_next(fillers)
    return y, s_new


def _gla_chunk(q, k, v, z, bcum, st_old, causal, g_gla):
    n = q.shape[0]
    bend = bcum[n - 1:n, :]
    qf = (q * jnp.exp(bcum)).astype(BF16)
    kf = (k * jnp.exp(-bcum)).astype(BF16)
    scores = jnp.where(causal, _dot_nt(qf, kf), 0.0).astype(BF16)
    o = _dot(scores, v) + _dot_nt(qf, st_old.astype(BF16))
    k_tail = (k * jnp.exp(bend - bcum)).astype(BF16)
    st_new = st_old * jnp.exp(bend) + _dot_tn(v, k_tail)
    return _rms(o, g_gla) * _silu(z), st_new


def _gla_chunk_anchored(q, k, v, z, gk, tril, st_old, g_gla, sub, fillers):
    fillers = list(fillers)
    n = q.shape[0]
    bcum = _cumsum_rows(gk, tril)
    _run_next(fillers)
    bend = bcum[n - 1:n, :]
    scores = []
    for j in range(n // sub):
        lo, hi = j * sub, (j + 1) * sub
        width = -(-hi // 128) * 128
        anchor = bcum[lo - 1:lo, :] if j else jnp.zeros_like(bend)
        qj = (q[lo:hi] * jnp.exp(bcum[lo:hi] - anchor)).astype(BF16)
        row = lax.broadcasted_iota(jnp.int32, (width, C_DK), 0)
        kj = (k[:width] * jnp.exp(jnp.where(row < hi, anchor - bcum[:width], 0.0))).astype(BF16)
        t_idx = lax.broadcasted_iota(jnp.int32, (sub, width), 0) + lo
        s_idx = lax.broadcasted_iota(jnp.int32, (sub, width), 1)
        scores.append(jnp.where(t_idx >= s_idx, _dot_nt(qj, kj), 0.0).astype(BF16))
    q_state = (q * jnp.exp(bcum)).astype(BF16)
    k_tail = (k * jnp.exp(bend - bcum)).astype(BF16)
    _run_next(fillers)
    o = jnp.concatenate([_dot(s, v[:s.shape[1]]) for s in scores], axis=0) + _dot_nt(q_state, st_old.astype(BF16))
    st_new = st_old * jnp.exp(bend) + _dot_tn(v, k_tail)
    _run_next(fillers)
    y = _rms(o, g_gla) * _silu(z)
    while fillers:
        _run_next(fillers)
    return y, st_new


def _cumsum_rows(g, tril):
    g_hi = g.astype(BF16)
    g_lo = (g - g_hi.astype(F32)).astype(BF16)
    return _dot(tril, g_hi) + _dot(tril, g_lo)


def _block_tril(n, blk):
    row = lax.broadcasted_iota(jnp.int32, (n, n), 0)
    col = lax.broadcasted_iota(jnp.int32, (n, n), 1)
    return jnp.where((row >= col) & (row // blk == col // blk), 1.0, 0.0).astype(BF16)


def _causal(n):
    return lax.broadcasted_iota(jnp.int32, (n, n), 0) >= lax.broadcasted_iota(jnp.int32, (n, n), 1)


def _alias_prev(in_specs, args, prev, out_index):
    if prev is None:
        return {}
    in_specs.append(pl.BlockSpec(memory_space=pl.ANY))
    args.append(prev)
    return {len(args) - 1: out_index}


def _mod_kernel(c_ref, w_ref, b_ref, o_ref):
    s = _silu(c_ref[...]).astype(BF16)
    o_ref[0] = _dot(s, w_ref[0].astype(BF16)) + b_ref[0]


def _modulation(c_all, w_mod, b_mod):
    rows = c_all.shape[0]
    tn = 1024
    n = 3 * D_MODEL
    return pl.pallas_call(
        _mod_kernel,
        grid=(DEPTH, n // tn),
        in_specs=[
            pl.BlockSpec((rows, D_MODEL), lambda l, j: (0, 0)),
            pl.BlockSpec((1, D_MODEL, tn), lambda l, j: (l, 0, j)),
            pl.BlockSpec((1, 1, tn), lambda l, j: (l, 0, j)),
        ],
        out_specs=pl.BlockSpec((1, rows, tn), lambda l, j: (l, 0, j)),
        out_shape=jax.ShapeDtypeStruct((DEPTH, rows, n), F32),
        compiler_params=_params("arbitrary", "arbitrary"),
        name="modulation",
    )(c_all, w_mod, b_mod.reshape(DEPTH, 1, n))


def _fused_even_kernel(h_ref, wu_ref, wva_ref, wza_ref, wq_ref, wk_ref, wv_ref, wz_ref,
                       cos_ref, sin_ref, wm_ref, bs_ref, lng_ref, retg_ref, *rest, tm, rc, aliased):
    rest = rest[1:] if aliased else rest
    ya_ref, yb_ref, sfin_ref, s_scr, dec_scr, qd_scr, kd_scr = rest
    b = pl.program_id(0)
    j = pl.program_id(1)
    i = pl.program_id(2)

    @pl.when((b == 0) & (j == 0) & (i == 0))
    def _():
        _fill_retention_tables(dec_scr, qd_scr, kd_scr, rc)

    n_chunks = tm // rc

    def projection(c, w_refs):
        rows = pl.ds(pl.multiple_of(i * tm + c * rc, rc), rc)
        p = [None] * len(w_refs)

        def piece(n):
            def run():
                p[n] = _dot(h_ref[rows, :], w_refs[n][0])
            return run

        return p, [piece(n) for n in range(len(w_refs))]

    @pl.when(j < A_HEADS)
    def _():
        w_refs = (wva_ref, wu_ref, wza_ref)
        cur, pieces = projection(0, w_refs)
        while pieces:
            _run_next(pieces)
        for c in range(n_chunks):
            nxt, pieces = projection(c + 1, w_refs) if c + 1 < n_chunks else (None, [])
            y, _ = _gmlp_unit(cur[1], cur[0], cur[2], lng_ref[0], wm_ref[0, 0], bs_ref[0, 0], GMLP_CHUNK, pieces)
            ya_ref[c * rc:(c + 1) * rc, :] = y.astype(ya_ref.dtype)
            cur = nxt

    @pl.when((j >= A_HEADS) & (i == 0))
    def _():
        s_scr[...] = jnp.zeros_like(s_scr)

    @pl.when(j >= A_HEADS)
    def _():
        hd = j - A_HEADS
        s = s_scr[...]
        dec = dec_scr[hd]
        qd = qd_scr[hd]
        kd = kd_scr[hd]
        w_refs = (wq_ref, wk_ref, wv_ref, wz_ref)
        cur, pieces = projection(0, w_refs)
        while pieces:
            _run_next(pieces)
        for c in range(n_chunks):
            rows = slice(c * rc, (c + 1) * rc)
            nxt, pieces = projection(c + 1, w_refs) if c + 1 < n_chunks else (None, [])
            y, s = _retention_chunk(cur[0], cur[1], cur[2], cur[3], s, cos_ref[rows, :], sin_ref[rows, :],
                                    dec, qd, kd, retg_ref[0], pieces)
            yb_ref[rows, :] = y.astype(yb_ref.dtype)
            cur = nxt
        s_scr[...] = s
        sfin_ref[0, 0, 0] = s


def _fused_even(h, w_in, cos, sin, wm, bs, ln_g3, ret_g3, prev_state, *, layer, batch, length):
    li = layer // 2
    tm, rc = PROMPT_TM, PROMPT_RC
    tpb = length // tm
    t = batch * length
    n_items = A_HEADS + B_HEADS

    def a_of(j):
        return jnp.minimum(j, A_HEADS - 1)

    def h_of(j):
        return jnp.maximum(j - A_HEADS, 0)

    def ya_row(b, j, i):
        return b * tpb + jnp.where(j < A_HEADS, i, tpb - 1)

    def yb_row(b, j, i):
        return b * tpb + jnp.where(j >= A_HEADS, i, 0)

    def wa_spec(seg):
        return pl.BlockSpec((1, D_MODEL, A_HEAD_DIM), lambda b, j, i: (li, 0, seg * A_HEADS + a_of(j)))

    def wb_spec(seg):
        base = 3 * A_WIDTH // B_DK + seg * B_HEADS
        return pl.BlockSpec((1, D_MODEL, B_DK), lambda b, j, i: (li, 0, base + h_of(j)))

    in_specs = [
        pl.BlockSpec((length, D_MODEL), lambda b, j, i: (b, 0)),
        wa_spec(0), wa_spec(1), wa_spec(2),
        wb_spec(0), wb_spec(1), wb_spec(2), wb_spec(3),
        pl.BlockSpec((tm, B_DK // 2), lambda b, j, i: (i, 0)),
        pl.BlockSpec((tm, B_DK // 2), lambda b, j, i: (i, 0)),
        pl.BlockSpec((1, 1, GMLP_CHUNK, GMLP_CHUNK), lambda b, j, i: (li, a_of(j), 0, 0)),
        pl.BlockSpec((1, 1, GMLP_CHUNK, A_HEAD_DIM), lambda b, j, i: (li, a_of(j), 0, 0)),
        pl.BlockSpec((1, 1, A_HEAD_DIM), lambda b, j, i: (li, 0, a_of(j))),
        pl.BlockSpec((1, 1, B_DV), lambda b, j, i: (li, 0, 0)),
    ]
    args = [h, w_in, w_in, w_in, w_in, w_in, w_in, w_in, cos, sin, wm, bs, ln_g3, ret_g3]
    aliases = _alias_prev(in_specs, args, prev_state, 2)
    n_state = DEPTH // 2
    return pl.pallas_call(
        functools.partial(_fused_even_kernel, tm=tm, rc=rc, aliased=prev_state is not None),
        grid=(batch, n_items, tpb),
        in_specs=in_specs,
        out_specs=[
            pl.BlockSpec((tm, A_HEAD_DIM), lambda b, j, i: (ya_row(b, j, i), a_of(j))),
            pl.BlockSpec((tm, B_DV), lambda b, j, i: (yb_row(b, j, i), h_of(j))),
            pl.BlockSpec((1, 1, 1, B_DK, B_DV), lambda b, j, i: (li, b, h_of(j), 0, 0)),
        ],
        out_shape=[
            jax.ShapeDtypeStruct((t, A_WIDTH), BF16),
            jax.ShapeDtypeStruct((t, B_WIDTH), BF16),
            jax.ShapeDtypeStruct((n_state, batch, B_HEADS, B_DK, B_DV), F32),
        ],
        scratch_shapes=[
            pltpu.VMEM((B_DK, B_DV), F32),
            pltpu.VMEM((B_HEADS, rc, rc), F32),
            pltpu.VMEM((B_HEADS, rc, B_DK // 2), F32),
            pltpu.VMEM((B_HEADS, rc, B_DK // 2), F32),
        ],
        input_output_aliases=aliases,
        compiler_params=_params("arbitrary", "arbitrary", "arbitrary"),
        name="fused_even",
    )(*args)


def _fused_odd_kernel(h_ref, wq_ref, wk_ref, wv_ref, wz_ref, wlr_ref, wgk_ref, bgk_ref,
                      glag_ref, *rest, tm, rc, blk, aliased):
    rest = rest[1:] if aliased else rest
    y_ref, sfin_ref, lr_scr, st_scr = rest
    hd = pl.program_id(1)
    i = pl.program_id(2)

    @pl.when(hd == 0)
    def _():
        lr_scr[i] = _dot(h_ref[pl.ds(pl.multiple_of(i * tm, tm), tm), :], wlr_ref[0]).astype(BF16)

    @pl.when(i == 0)
    def _():
        st_scr[...] = jnp.zeros_like(st_scr)

    def projection(c):
        rows = slice(c * rc, (c + 1) * rc)
        grows = pl.ds(pl.multiple_of(i * tm + c * rc, rc), rc)
        p = {}

        def piece_q():
            p["q"] = _dot(h_ref[grows, :], wq_ref[0]) * (C_DK ** -0.5)
            p["gk"] = _log_sigmoid(_dot(lr_scr[i, rows, :], wgk_ref[0]) + bgk_ref[0]) * (1.0 / GLA_GATE_NORM)

        def piece_k():
            p["k"] = _dot(h_ref[grows, :], wk_ref[0])

        def piece_v():
            p["v"] = _dot(h_ref[grows, :], wv_ref[0]).astype(BF16)

        def piece_z():
            p["z"] = _dot(h_ref[grows, :], wz_ref[0])

        return p, [piece_q, piece_k, piece_v, piece_z]

    tril = _block_tril(rc, rc)
    st = st_scr[...]
    n_chunks = tm // rc
    cur, pieces = projection(0)
    while pieces:
        _run_next(pieces)
    for c in range(n_chunks):
        nxt, pieces = projection(c + 1) if c + 1 < n_chunks else (None, [])
        y, st = _gla_chunk_anchored(cur["q"], cur["k"], cur["v"], cur["z"], cur["gk"], tril, st, glag_ref[0], blk,
                                    pieces)
        y_ref[c * rc:(c + 1) * rc, :] = y.astype(y_ref.dtype)
        cur = nxt
    st_scr[...] = st
    sfin_ref[0, 0, 0] = st.T


def _fused_odd(h, w_in, w_lr, w_gk, b_gk3, gla_g3, prev_state, *, layer, batch, length):
    li = layer // 2
    tm, rc, blk = PROMPT_TM, PROMPT_RC, PROMPT_GLA_BLK
    tpb = length // tm
    t = batch * length
    v_base = 2 * C_HEADS * C_DK // C_DV

    in_specs = [
        pl.BlockSpec((length, D_MODEL), lambda b, h, i: (b, 0)),
        pl.BlockSpec((1, D_MODEL, C_DK), lambda b, h, i: (li, 0, h)),
        pl.BlockSpec((1, D_MODEL, C_DK), lambda b, h, i: (li, 0, C_HEADS + h)),
        pl.BlockSpec((1, D_MODEL, C_DV), lambda b, h, i: (li, 0, v_base + h)),
        pl.BlockSpec((1, D_MODEL, C_DV), lambda b, h, i: (li, 0, v_base + C_HEADS + h)),
        pl.BlockSpec((1, D_MODEL, LR_PAD), lambda b, h, i: (li, 0, 0)),
        pl.BlockSpec((1, LR_PAD, C_DK), lambda b, h, i: (li, 0, h)),
        pl.BlockSpec((1, 1, C_DK), lambda b, h, i: (li, 0, h)),
        pl.BlockSpec((1, 1, C_DV), lambda b, h, i: (li, 0, 0)),
    ]
    args = [h, w_in, w_in, w_in, w_in, w_lr, w_gk, b_gk3, gla_g3]
    aliases = _alias_prev(in_specs, args, prev_state, 1)
    n_state = DEPTH // 2
    return pl.pallas_call(
        functools.partial(_fused_odd_kernel, tm=tm, rc=rc, blk=blk, aliased=prev_state is not None),
        grid=(batch, C_HEADS, tpb),
        in_specs=in_specs,
        out_specs=[
            pl.BlockSpec((tm, C_DV), lambda b, h, i: (b * tpb + i, h)),
            pl.BlockSpec((1, 1, 1, C_DK, C_DV), lambda b, h, i: (li, b, h, 0, 0)),
        ],
        out_shape=[
            jax.ShapeDtypeStruct((t, C_WIDTH), BF16),
            jax.ShapeDtypeStruct((n_state, batch, C_HEADS, C_DK, C_DV), F32),
        ],
        scratch_shapes=[
            pltpu.VMEM((tpb, tm, LR_PAD), BF16),
            pltpu.VMEM((C_DV, C_DK), F32),
        ],
        input_output_aliases=aliases,
        compiler_params=_params("arbitrary", "arbitrary", "arbitrary"),
        name="fused_odd",
    )(*args)


def _first_norm_kernel(x_ref, sh_ref, sc_ref, g_ref, h_ref):
    h_ref[...] = _norm_modulate(x_ref[...], g_ref[0], sc_ref[...], sh_ref[...])


def _first_norm(x, mod5, norm_g3, *, length):
    t, d = x.shape
    tm = OUT_NORM_TM
    tpb = length // tm

    def mod_spec(which):
        return pl.BlockSpec((None, None, None, 1, d), lambda i: (0, i // tpb, which, 0, 0))

    return pl.pallas_call(
        _first_norm_kernel,
        grid=(t // tm,),
        in_specs=[pl.BlockSpec((tm, d), lambda i: (i, 0)), mod_spec(0), mod_spec(1),
                  pl.BlockSpec((1, 1, d), lambda i: (0, 0, 0))],
        out_specs=pl.BlockSpec((tm, d), lambda i: (i, 0)),
        out_shape=jax.ShapeDtypeStruct((t, d), BF16),
        compiler_params=_params("arbitrary"),
        name="first_norm",
    )(x, mod5, mod5, norm_g3)


def _out_norm_kernel(*refs, n_parts, n_chunks, last):
    y_refs = refs[:n_parts]
    w_refs = refs[n_parts:2 * n_parts]
    if last:
        x_ref, gate_ref, g_ref, o_ref, xn_ref = refs[2 * n_parts:]
    else:
        x_ref, gate_ref, g_ref, sh_ref, sc_ref, xn_ref, h_ref = refs[2 * n_parts:]
    d = x_ref.shape[-1]
    tn = d // n_chunks

    def residual(n, acc):
        cols = slice(n * tn, (n + 1) * tn)
        xn = x_ref[:, cols] + gate_ref[:, cols] * acc
        xn_ref[:, cols] = xn
        return jnp.sum(xn * xn, axis=-1, keepdims=True)

    ssq = 0.0
    pending = None
    for n in range(n_chunks):
        cols = slice(n * tn, (n + 1) * tn)
        acc = _dot(y_refs[0][...], w_refs[0][0, :, cols])
        for p in range(1, n_parts):
            acc = acc + _dot(y_refs[p][...], w_refs[p][0, :, cols])
        if pending is not None:
            ssq = ssq + residual(*pending)
        pending = (n, acc)
    ssq = ssq + residual(*pending)
    r = lax.rsqrt(ssq * (1.0 / d) + EPS)
    for n in range(n_chunks):
        cols = slice(n * tn, (n + 1) * tn)
        y = xn_ref[:, cols] * r * g_ref[0][:, cols]
        if last:
            o_ref[:, cols] = y
        else:
            h_ref[:, cols] = (y * (1.0 + sc_ref[:, cols]) + sh_ref[:, cols]).astype(h_ref.dtype)


def _out_norm(y_parts, w, x, mod5, norm_g3, final_g, *, layer, length):
    li = layer // 2
    last = layer == DEPTH - 1
    t, d = x.shape
    tm = OUT_NORM_TM
    tpb = length // tm
    n_parts = len(y_parts)
    kp = w.shape[1] // n_parts

    def mod_spec(lyr, which):
        return pl.BlockSpec((None, None, None, 1, d), lambda i: (lyr, i // tpb, which, 0, 0))

    row_spec = pl.BlockSpec((tm, d), lambda i: (i, 0))
    in_specs = [pl.BlockSpec((tm, kp), lambda i: (i, 0)) for _ in range(n_parts)]
    in_specs += [pl.BlockSpec((1, kp, d), functools.partial(lambda i, p: (li, p, 0), p=p),
                              pipeline_mode=pl.Buffered(1)) for p in range(n_parts)]
    in_specs += [row_spec, mod_spec(layer, 2)]
    args = [*y_parts, *([w] * n_parts), x, mod5]
    if last:
        in_specs.append(pl.BlockSpec((1, 1, d), lambda i: (0, 0, 0)))
        args.append(final_g.reshape(1, 1, d))
        out_specs = [row_spec]
        out_shape = [jax.ShapeDtypeStruct((t, d), F32)]
        scratch = [pltpu.VMEM((tm, d), F32)]
    else:
        in_specs += [pl.BlockSpec((1, 1, d), lambda i: (layer + 1, 0, 0)),
                     mod_spec(layer + 1, 0), mod_spec(layer + 1, 1)]
        args += [norm_g3, mod5, mod5]
        out_specs = [row_spec, row_spec]
        out_shape = [jax.ShapeDtypeStruct((t, d), F32), jax.ShapeDtypeStruct((t, d), BF16)]
        scratch = []
    return pl.pallas_call(
        functools.partial(_out_norm_kernel, n_parts=n_parts, n_chunks=d // OUT_TN, last=last),
        grid=(t // tm,),
        in_specs=in_specs,
        out_specs=out_specs,
        out_shape=out_shape,
        scratch_shapes=scratch,
        compiler_params=_params("arbitrary"),
        name="out_norm",
    )(*args)


def _out_proj_kernel(*refs, n_parts):
    y_refs = refs[:n_parts]
    w_refs = refs[n_parts:2 * n_parts]
    x_ref, gate_ref, o_ref = refs[2 * n_parts:]
    acc = _dot(y_refs[0][...], w_refs[0][0])
    for p in range(1, n_parts):
        acc = acc + _dot(y_refs[p][...], w_refs[p][0])
    o_ref[...] = x_ref[...] + gate_ref[...] * acc


def _out_proj(y_parts, w, x, gate_arr, gate_spec, *, layer, tm):
    li = layer // 2
    t, d = x.shape
    tn = OUT_TN
    n_parts = len(y_parts)
    kp = w.shape[1] // n_parts
    in_specs = [pl.BlockSpec((tm, kp), lambda i, j: (i, 0)) for _ in range(n_parts)]
    in_specs += [pl.BlockSpec((1, kp, tn), functools.partial(lambda i, j, p: (li, p, j), p=p)) for p in range(n_parts)]
    in_specs += [pl.BlockSpec((tm, tn), lambda i, j: (i, j)), gate_spec]
    return pl.pallas_call(
        functools.partial(_out_proj_kernel, n_parts=n_parts),
        grid=(t // tm, d // tn),
        in_specs=in_specs,
        out_specs=pl.BlockSpec((tm, tn), lambda i, j: (i, j)),
        out_shape=jax.ShapeDtypeStruct((t, d), F32),
        compiler_params=_params("arbitrary", "arbitrary"),
        name="out_proj",
    )(*y_parts, *([w] * n_parts), x, gate_arr)


def _final_norm_kernel(x_ref, g_ref, o_ref):
    o_ref[...] = _rms(x_ref[...], g_ref[...])


def _final_norm(x, g, *, tm):
    t, d = x.shape
    return pl.pallas_call(
        _final_norm_kernel,
        grid=(t // tm,),
        in_specs=[pl.BlockSpec((tm, d), lambda i: (i, 0)), pl.BlockSpec((1, d), lambda i: (0, 0))],
        out_specs=pl.BlockSpec((tm, d), lambda i: (i, 0)),
        out_shape=jax.ShapeDtypeStruct((t, d), F32),
        compiler_params=_params("arbitrary"),
        name="final_norm",
    )(x, g)


def _in_proj_kernel(x_ref, sh_ref, sc_ref, g_ref, w_ref, *rest, has_lr):
    if has_lr:
        wlr_ref, o_ref, lr_ref, h_scr = rest
    else:
        o_ref, h_scr = rest

    @pl.when(pl.program_id(0) == 0)
    def _():
        h = _norm_modulate(x_ref[...], g_ref[0], sc_ref[...], sh_ref[...])
        h_scr[...] = h
        if has_lr:
            lr_ref[...] = _dot(h, wlr_ref[0]).astype(BF16)

    o_ref[...] = _dot(h_scr[...], w_ref[0]).astype(o_ref.dtype)


def _in_proj(x, mod_rows, norm_g3, w, wlr, *, layer):
    li = layer // 2
    t, d = x.shape
    tn = PROJ_TN
    n = w.shape[2] // tn * tn
    has_lr = wlr is not None
    in_specs = [
        pl.BlockSpec((t, d), lambda j: (0, 0)),
        pl.BlockSpec((None, None, t, d), lambda j: (layer, 0, 0, 0)),
        pl.BlockSpec((None, None, t, d), lambda j: (layer, 1, 0, 0)),
        pl.BlockSpec((1, 1, d), lambda j: (layer, 0, 0)),
        pl.BlockSpec((1, d, tn), lambda j: (li, 0, j)),
    ]
    args = [x, mod_rows, mod_rows, norm_g3, w]
    out_specs = [pl.BlockSpec((t, tn), lambda j: (0, j))]
    out_shape = [jax.ShapeDtypeStruct((t, n), BF16)]
    if has_lr:
        in_specs.append(pl.BlockSpec((1, d, LR_PAD), lambda j: (li, 0, 0)))
        args.append(wlr)
        out_specs.append(pl.BlockSpec((t, LR_PAD), lambda j: (0, 0)))
        out_shape.append(jax.ShapeDtypeStruct((t, LR_PAD), BF16))
    return pl.pallas_call(
        functools.partial(_in_proj_kernel, has_lr=has_lr),
        grid=(n // tn,),
        in_specs=in_specs,
        out_specs=out_specs,
        out_shape=out_shape,
        scratch_shapes=[pltpu.VMEM((t, d), BF16)],
        compiler_params=_params("arbitrary"),
        name="in_proj",
    )(*args)


def _even_mix_kernel(p_ref, cos_ref, sin_ref, wm_ref, bs_ref, lng_ref, retg_ref, s0_ref, *rest, ts, aliased):
    rest = rest[2:] if aliased else rest
    y_ref, sfin_ref, va_ref, dec_scr, qd_scr, kd_scr = rest

    @pl.when(pl.program_id(0) == 0)
    def _():
        _fill_retention_tables(dec_scr, qd_scr, kd_scr, ts)

    for a in range(A_HEADS):
        c0 = a * A_HEAD_DIM
        u = p_ref[:, c0:c0 + A_HEAD_DIM].astype(F32)
        v = p_ref[:, A_WIDTH + c0:A_WIDTH + c0 + A_HEAD_DIM].astype(F32)
        z = p_ref[:, 2 * A_WIDTH + c0:2 * A_WIDTH + c0 + A_HEAD_DIM].astype(F32)
        y, vn = _gmlp_unit(u, v, z, lng_ref[0, :, c0:c0 + A_HEAD_DIM], wm_ref[0, a], bs_ref[0, a], ts)
        va_ref[0, :, c0:c0 + A_HEAD_DIM] = vn
        y_ref[:, c0:c0 + A_HEAD_DIM] = y.astype(y_ref.dtype)

    q0 = 3 * A_WIDTH
    k0 = q0 + B_HEADS * B_DK
    v0 = k0 + B_HEADS * B_DK
    z0 = v0 + B_WIDTH
    cos = cos_ref[...]
    sin = sin_ref[...]
    for h in range(B_HEADS):
        hc = h * B_DK
        y, s_new = _retention_chunk(
            p_ref[:, q0 + hc:q0 + hc + B_DK].astype(F32), p_ref[:, k0 + hc:k0 + hc + B_DK].astype(F32),
            p_ref[:, v0 + hc:v0 + hc + B_DV].astype(F32), p_ref[:, z0 + hc:z0 + hc + B_DV].astype(F32),
            s0_ref[0, 0, h], cos, sin, dec_scr[h], qd_scr[h], kd_scr[h], retg_ref[0])
        sfin_ref[0, 0, h] = s_new
        y_ref[:, A_WIDTH + hc:A_WIDTH + hc + B_DV] = y.astype(y_ref.dtype)


def _even_mix(p, cos, sin, wm, bs, ln_g3, ret_g3, state, prev_state, prev_va, *, layer, batch, ts):
    li = layer // 2
    n_state = DEPTH // 2
    in_specs = [
        pl.BlockSpec((ts, E_IN), lambda b: (b, 0)),
        pl.BlockSpec((ts, B_DK // 2), lambda b: (0, 0)),
        pl.BlockSpec((ts, B_DK // 2), lambda b: (0, 0)),
        pl.BlockSpec((1, A_HEADS, ts, ts), lambda b: (li, 0, 0, 0)),
        pl.BlockSpec((1, A_HEADS, ts, A_HEAD_DIM), lambda b: (li, 0, 0, 0)),
        pl.BlockSpec((1, 1, A_WIDTH), lambda b: (li, 0, 0)),
        pl.BlockSpec((1, 1, B_DV), lambda b: (li, 0, 0)),
        pl.BlockSpec((1, 1, B_HEADS, B_DK, B_DV), lambda b: (li, b, 0, 0, 0)),
    ]
    args = [p, cos, sin, wm, bs, ln_g3, ret_g3, state]
    aliases = _alias_prev(in_specs, args, prev_state, 1)
    aliases.update(_alias_prev(in_specs, args, prev_va, 2))
    return pl.pallas_call(
        functools.partial(_even_mix_kernel, ts=ts, aliased=prev_state is not None),
        grid=(batch,),
        in_specs=in_specs,
        out_specs=[
            pl.BlockSpec((ts, E_MIX), lambda b: (b, 0)),
            pl.BlockSpec((1, 1, B_HEADS, B_DK, B_DV), lambda b: (li, b, 0, 0, 0)),
            pl.BlockSpec((1, ts, A_WIDTH), lambda b: (li, b, 0)),
        ],
        out_shape=[
            jax.ShapeDtypeStruct((batch * ts, E_MIX), BF16),
            jax.ShapeDtypeStruct((n_state, batch, B_HEADS, B_DK, B_DV), F32),
            jax.ShapeDtypeStruct((n_state, batch * ts, A_WIDTH), F32),
        ],
        scratch_shapes=[
            pltpu.VMEM((B_HEADS, ts, ts), F32),
            pltpu.VMEM((B_HEADS, ts, B_DK // 2), F32),
            pltpu.VMEM((B_HEADS, ts, B_DK // 2), F32),
        ],
        input_output_aliases=aliases,
        compiler_params=_params("arbitrary"),
        name="even_mix",
    )(*args)


def _odd_mix_kernel(p_ref, lr_ref, wgk_ref, bgk_ref, glag_ref, s0_ref, *rest, ts, aliased):
    rest = rest[1:] if aliased else rest
    y_ref, sfin_ref = rest
    gk = _log_sigmoid(_dot(lr_ref[...], wgk_ref[0]) + bgk_ref[0]) * (1.0 / GLA_GATE_NORM)
    tril = _block_tril(ts, ts)
    causal = _causal(ts)
    k0 = C_HEADS * C_DK
    v0 = 2 * C_HEADS * C_DK
    z0 = v0 + C_WIDTH
    for h in range(C_HEADS):
        kc = h * C_DK
        vc = h * C_DV
        bcum = _cumsum_rows(gk[:, kc:kc + C_DK], tril)
        q = p_ref[:, kc:kc + C_DK].astype(F32) * (C_DK ** -0.5)
        k = p_ref[:, k0 + kc:k0 + kc + C_DK].astype(F32)
        v = p_ref[:, v0 + vc:v0 + vc + C_DV]
        z = p_ref[:, z0 + vc:z0 + vc + C_DV].astype(F32)
        y, st = _gla_chunk(q, k, v, z, bcum, s0_ref[0, 0, h].T, causal, glag_ref[0])
        sfin_ref[0, 0, h] = st.T
        y_ref[:, vc:vc + C_DV] = y.astype(y_ref.dtype)


def _odd_mix(p, lr, w_gk, b_gk3, gla_g3, state, prev_state, *, layer, batch, ts):
    li = layer // 2
    n_state = DEPTH // 2
    in_specs = [
        pl.BlockSpec((ts, O_MAIN), lambda b: (b, 0)),
        pl.BlockSpec((ts, LR_PAD), lambda b: (b, 0)),
        pl.BlockSpec((1, LR_PAD, C_HEADS * C_DK), lambda b: (li, 0, 0)),
        pl.BlockSpec((1, 1, C_HEADS * C_DK), lambda b: (li, 0, 0)),
        pl.BlockSpec((1, 1, C_DV), lambda b: (li, 0, 0)),
        pl.BlockSpec((1, 1, C_HEADS, C_DK, C_DV), lambda b: (li, b, 0, 0, 0)),
    ]
    args = [p, lr, w_gk, b_gk3, gla_g3, state]
    aliases = _alias_prev(in_specs, args, prev_state, 1)
    return pl.pallas_call(
        functools.partial(_odd_mix_kernel, ts=ts, aliased=prev_state is not None),
        grid=(batch,),
        in_specs=in_specs,
        out_specs=[
            pl.BlockSpec((ts, C_WIDTH), lambda b: (b, 0)),
            pl.BlockSpec((1, 1, C_HEADS, C_DK, C_DV), lambda b: (li, b, 0, 0, 0)),
        ],
        out_shape=[
            jax.ShapeDtypeStruct((batch * ts, C_WIDTH), BF16),
            jax.ShapeDtypeStruct((n_state, batch, C_HEADS, C_DK, C_DV), F32),
        ],
        input_output_aliases=aliases,
        compiler_params=_params("arbitrary"),
        name="odd_mix",
    )(*args)


def _rope_tables(pos):
    half = B_DK // 2
    freqs = ROPE_BASE ** (-jnp.arange(half, dtype=F32) / half)
    ang = pos.astype(F32)[:, None] * freqs[None, :]
    return jnp.cos(ang), jnp.sin(ang)


def _gmlp_weights(gmlp_ws, gmlp_bs, seg):
    p = np.arange(seg)
    mask = (p[None, :] // CHUNK) <= (p[:, None] // CHUNK)
    wm = jnp.where(mask[None, None], gmlp_ws[:, :, :seg, :seg], 0.0).astype(BF16)
    bs = jnp.broadcast_to(gmlp_bs[:, :, :seg, None], gmlp_bs.shape[:2] + (seg, A_HEAD_DIM))
    return wm, bs


def _prompt_trunk(x, mod5, w, *, batch, length):
    cos, sin = _rope_tables(jnp.arange(length))
    wm, bs = _gmlp_weights(w["gmlp_ws"], w["gmlp_bs"], GMLP_CHUNK)
    ret, gla = None, None
    h = _first_norm(x, mod5, w["norm_g"], length=length)
    for layer in range(DEPTH):
        if layer % 2 == 0:
            ya, yb, ret = _fused_even(h, w["w_in_e"], cos, sin, wm, bs, w["gmlp_ln_g"], w["ret_norm_g"], ret,
                                      layer=layer, batch=batch, length=length)
            outs = _out_norm([ya, yb], w["w_out_e"], x, mod5, w["norm_g"], w["final_g"], layer=layer, length=length)
        else:
            y, gla = _fused_odd(h, w["w_in_o"], w["w_lr"], w["w_gk"], w["b_gk"], w["gla_norm_g"], gla,
                                layer=layer, batch=batch, length=length)
            outs = _out_norm([y], w["w_out_o"], x, mod5, w["norm_g"], w["final_g"], layer=layer, length=length)
        if layer < DEPTH - 1:
            x, h = outs
    return outs[0], ret, gla


def _sample_trunk(x, mod_rows, state_ret, state_gla, w, *, batch, length):
    t = batch * length
    cos, sin = _rope_tables(PAST_LEN + jnp.arange(length))
    wm, bs = _gmlp_weights(w["gmlp_ws"], w["gmlp_bs"], length)
    gate_spec = lambda layer: pl.BlockSpec((None, None, t, OUT_TN), lambda i, j: (layer, 2, 0, j))
    ret, gla, va = None, None, None
    for layer in range(DEPTH):
        if layer % 2 == 0:
            (p,) = _in_proj(x, mod_rows, w["norm_g"], w["w_in_e"], None, layer=layer)
            y, ret, va = _even_mix(p, cos, sin, wm, bs, w["gmlp_ln_g"], w["ret_norm_g"], state_ret, ret, va,
                                   layer=layer, batch=batch, ts=length)
            x = _out_proj([y], w["w_out_e"], x, mod_rows, gate_spec(layer), layer=layer, tm=t)
        else:
            p, lr = _in_proj(x, mod_rows, w["norm_g"], w["w_in_o"], w["w_lr"], layer=layer)
            y, gla = _odd_mix(p, lr, w["w_gk"], w["b_gk"], w["gla_norm_g"], state_gla, gla,
                              layer=layer, batch=batch, ts=length)
            x = _out_proj([y], w["w_out_o"], x, mod_rows, gate_spec(layer), layer=layer, tm=t)
    return _final_norm(x, w["final_g"], tm=t), ret, gla, va


def kernel(x_prompt, x_sample, state_ret, state_gla, c_prompt, c_sample, norm_g, w_mod, b_mod, w_in_e, gmlp_ln_g,
           gmlp_ws, gmlp_bs, ret_norm_g, w_out_e, w_in_o, w_gk2, b_gk, gla_norm_g, w_out_o, final_g):
    bp, lp, d = x_prompt.shape
    bs_, ls = x_sample.shape[:2]
    n_even, n_odd = w_in_e.shape[0], w_in_o.shape[0]

    w = {
        "norm_g": norm_g.reshape(DEPTH, 1, d),
        "w_in_e": w_in_e.astype(BF16),
        "w_out_e": w_out_e.astype(BF16),
        "w_in_o": w_in_o.astype(BF16),
        "w_lr": jnp.pad(w_in_o[:, :, O_MAIN:], ((0, 0), (0, 0), (0, LR_PAD - GLA_RANK))).astype(BF16),
        "w_gk": jnp.pad(w_gk2, ((0, 0), (0, LR_PAD - GLA_RANK), (0, 0))).astype(BF16),
        "w_out_o": w_out_o.astype(BF16),
        "gmlp_ws": gmlp_ws, "gmlp_bs": gmlp_bs,
        "gmlp_ln_g": gmlp_ln_g.reshape(n_even, 1, A_WIDTH),
        "ret_norm_g": ret_norm_g.reshape(n_even, 1, B_DV),
        "b_gk": b_gk.reshape(n_odd, 1, C_HEADS * C_DK),
        "gla_norm_g": gla_norm_g.reshape(n_odd, 1, C_DV),
        "final_g": final_g.reshape(1, d),
    }

    c_all = jnp.concatenate([c_prompt, c_sample], axis=0)
    mod = _modulation(c_all, w_mod, b_mod)
    mod5 = mod.reshape(DEPTH, bp + bs_, 3, 1, d)
    mod_rows = jnp.repeat(mod[:, bp:].reshape(DEPTH, bs_, 3, d).transpose(0, 2, 1, 3), ls, axis=2)

    y_p, ret_p, gla_p = _prompt_trunk(x_prompt.reshape(bp * lp, d), mod5, w, batch=bp, length=lp)
    y_s, ret_s, gla_s, va = _sample_trunk(x_sample.reshape(bs_ * ls, d), mod_rows, state_ret, state_gla, w,
                                          batch=bs_, length=ls)
    return (y_p.reshape(bp, lp, d), y_s.reshape(bs_, ls, d), ret_p, ret_s, gla_p, gla_s,
            va.reshape(n_even, bs_, ls, A_WIDTH))
```

```python
import functools
import math

import jax
import jax.numpy as jnp
import numpy as np
from jax import lax
from jax.experimental import pallas as pl
from jax.experimental.pallas import tpu as pltpu

F32 = jnp.float32
BF16 = jnp.bfloat16

D_MODEL = 2048
DEPTH = 4
PAST_LEN = 1024
EPS = 1e-6
CHUNK = 64
A_HEADS = 4
A_HEAD_DIM = 512
A_WIDTH = 2048
GMLP_CHUNK = 128
B_HEADS = 8
B_DK = 256
B_DV = 256
B_WIDTH = 2048
ROPE_BASE = 10000.0
C_HEADS = 8
C_DK = 256
C_DV = 512
C_WIDTH = 4096
GLA_RANK = 16
GLA_GATE_NORM = 16.0
E_IN = 3 * A_WIDTH + 2 * B_HEADS * B_DK + 2 * B_WIDTH
E_MIX = A_WIDTH + B_WIDTH
O_MAIN = 2 * C_HEADS * C_DK + 2 * C_WIDTH
LR_PAD = 128

VMEM_LIMIT_BYTES = 56 * 1024 * 1024

PROMPT_TM = 2048
PROMPT_RC = 256
PROMPT_GLA_BLK = 64
PROJ_TN = 1024
OUT_TN = 512
OUT_NORM_TM = 512


def _params(*sem):
    return pltpu.CompilerParams(dimension_semantics=sem, vmem_limit_bytes=VMEM_LIMIT_BYTES)


def _dot(a, b):
    return jnp.dot(a, b, preferred_element_type=F32)


def _dot_nt(a, b):
    return lax.dot_general(a, b, (((1,), (1,)), ((), ())), preferred_element_type=F32)


def _dot_tn(a, b):
    return lax.dot_general(a, b, (((0,), (0,)), ((), ())), preferred_element_type=F32)


def _silu(x):
    return x * jax.nn.sigmoid(x)


def _log_sigmoid(x):
    return jnp.minimum(x, 0.0) - jnp.log1p(jnp.exp(-jnp.abs(x)))


def _rms(x, g):
    return x * lax.rsqrt(jnp.mean(x * x, axis=-1, keepdims=True) + EPS) * g


def _norm_modulate(x, g, scale, shift):
    return (_rms(x, g) * (1.0 + scale) + shift).astype(BF16)


def _retention_log_gamma():
    return [math.log1p(-(2.0 ** (-5.0 - h))) for h in range(B_HEADS)]


def _fill_retention_tables(dec_scr, qd_scr, kd_scr, n):
    log_g = _retention_log_gamma()
    row = lax.broadcasted_iota(jnp.int32, (n, n), 0).astype(F32)
    col = lax.broadcasted_iota(jnp.int32, (n, n), 1).astype(F32)
    diff = row - col
    rowd = lax.broadcasted_iota(jnp.int32, (n, B_DK // 2), 0).astype(F32)
    for h in range(B_HEADS):
        dec_scr[h] = jnp.where(diff >= 0, jnp.exp(jnp.maximum(diff, 0.0) * log_g[h]), 0.0)
        qd_scr[h] = jnp.exp((rowd + 1.0) * log_g[h])
        kd_scr[h] = jnp.exp((n - 1.0 - rowd) * log_g[h])


def _rotary(xh, cos, sin):
    half = xh.shape[-1] // 2
    x1 = xh[:, :half]
    x2 = xh[:, half:]
    return jnp.concatenate([x1 * cos - x2 * sin, x1 * sin + x2 * cos], axis=-1)


def _run_next(fillers):
    if fillers:
        fillers.pop(0)()


def _gmlp_unit(u, v, z, g_ln, wm, bias, seg, fillers=()):
    fillers = list(fillers)
    v = jax.nn.gelu(v)
    vc = v - jnp.mean(v, axis=-1, keepdims=True)
    vn = vc * lax.rsqrt(jnp.mean(vc * vc, axis=-1, keepdims=True) + EPS) * g_ln
    vb = vn.astype(BF16)
    _run_next(fillers)
    mixed = [_dot(wm, vb[s * seg:(s + 1) * seg]) + bias for s in range(u.shape[0] // seg)]
    mixed = mixed[0] if len(mixed) == 1 else jnp.concatenate(mixed, axis=0)
    _run_next(fillers)
    y = jax.nn.gelu(u) * mixed * _silu(z)
    while fillers:
        _run_next(fillers)
    return y, vn


def _retention_chunk(q, k, v, z, s_old, cos, sin, dec, qd, kd, g_ret, fillers=()):
    fillers = list(fillers)
    n = q.shape[0]
    qd = jnp.concatenate([qd, qd], axis=-1)
    kd = jnp.concatenate([kd, kd], axis=-1)
    q = _rotary(q, cos, sin)
    k = _rotary(k, cos, sin) * (B_DK ** -0.5)
    vb = v.astype(BF16)
    _run_next(fillers)
    scores = _dot_nt(q.astype(BF16), k.astype(BF16)) * dec
    s_new = qd[n - 1:n, :] * s_old + _dot_tn((k * kd).astype(BF16), vb)
    _run_next(fillers)
    o = _dot(scores.astype(BF16), vb) + _dot((q * qd).astype(BF16), s_old.astype(BF16))
    _run_next(fillers)
    y = _rms(o, g_ret) * _silu(z)
    while fillers:
        _run_next(fillers)
    return y, s_new


def _gla_chunk(q, k, v, z, bcum, st_old, causal, g_gla):
    n = q.shape[0]
    bend = bcum[n - 1:n, :]
    qf = (q * jnp.exp(bcum)).astype(BF16)
    kf = (k * jnp.exp(-bcum)).astype(BF16)
    scores = jnp.where(causal, _dot_nt(qf, kf), 0.0).astype(BF16)
    o = _dot(scores, v) + _dot_nt(qf, st_old.astype(BF16))
    k_tail = (k * jnp.exp(bend - bcum)).astype(BF16)
    st_new = st_old * jnp.exp(bend) + _dot_tn(v, k_tail)
    return _rms(o, g_gla) * _silu(z), st_new


def _gla_chunk_anchored(q, k, v, z, gk, tril, st_old, g_gla, sub, fillers):
    fillers = list(fillers)
    n = q.shape[0]
    bcum = _cumsum_rows(gk, tril)
    _run_next(fillers)
    bend = bcum[n - 1:n, :]
    scores = []
    for j in range(n // sub):
        lo, hi = j * sub, (j + 1) * sub
        width = -(-hi // 128) * 128
        anchor = bcum[lo - 1:lo, :] if j else jnp.zeros_like(bend)
        qj = (q[lo:hi] * jnp.exp(bcum[lo:hi] - anchor)).astype(BF16)
        row = lax.broadcasted_iota(jnp.int32, (width, C_DK), 0)
        kj = (k[:width] * jnp.exp(jnp.where(row < hi, anchor - bcum[:width], 0.0))).astype(BF16)
        t_idx = lax.broadcasted_iota(jnp.int32, (sub, width), 0) + lo
        s_idx = lax.broadcasted_iota(jnp.int32, (sub, width), 1)
        scores.append(jnp.where(t_idx >= s_idx, _dot_nt(qj, kj), 0.0).astype(BF16))
    q_state = (q * jnp.exp(bcum)).astype(BF16)
    k_tail = (k * jnp.exp(bend - bcum)).astype(BF16)
    _run_next(fillers)
    o = jnp.concatenate([_dot(s, v[:s.shape[1]]) for s in scores], axis=0) + _dot_nt(q_state, st_old.astype(BF16))
    st_new = st_old * jnp.exp(bend) + _dot_tn(v, k_tail)
    _run_next(fillers)
    y = _rms(o, g_gla) * _silu(z)
    while fillers:
        _run_next(fillers)
    return y, st_new


def _cumsum_rows(g, tril):
    g_hi = g.astype(BF16)
    g_lo = (g - g_hi.astype(F32)).astype(BF16)
    return _dot(tril, g_hi) + _dot(tril, g_lo)


def _block_tril(n, blk):
    row = lax.broadcasted_iota(jnp.int32, (n, n), 0)
    col = lax.broadcasted_iota(jnp.int32, (n, n), 1)
    return jnp.where((row >= col) & (row // blk == col // blk), 1.0, 0.0).astype(BF16)


def _causal(n):
    return lax.broadcasted_iota(jnp.int32, (n, n), 0) >= lax.broadcasted_iota(jnp.int32, (n, n), 1)


def _alias_prev(in_specs, args, prev, out_index):
    if prev is None:
        return {}
    in_specs.append(pl.BlockSpec(memory_space=pl.ANY))
    args.append(prev)
    return {len(args) - 1: out_index}


def _stacked_out_spec(n_stack, slot, first, tail_shape, tail_index):
    if first:
        return pl.BlockSpec((n_stack,) + tail_shape, lambda *g: (0,) + tail_index(*g))
    return pl.BlockSpec((1,) + tail_shape, lambda *g: (slot,) + tail_index(*g))


def _zero_other_slots(ref):
    if ref.shape[0] > 1:
        ref[1:] = jnp.zeros((ref.shape[0] - 1,) + ref.shape[1:], ref.dtype)


def _mod_kernel(c_ref, w_ref, b_ref, o_ref):
    s = _silu(c_ref[...]).astype(BF16)
    o_ref[0] = _dot(s, w_ref[0].astype(BF16)) + b_ref[0]


def _modulation(c_all, w_mod, b_mod):
    rows = c_all.shape[0]
    tn = 1024
    n = 3 * D_MODEL
    return pl.pallas_call(
        _mod_kernel,
        grid=(DEPTH, n // tn),
        in_specs=[
            pl.BlockSpec((rows, D_MODEL), lambda l, j: (0, 0)),
            pl.BlockSpec((1, D_MODEL, tn), lambda l, j: (l, 0, j)),
            pl.BlockSpec((1, 1, tn), lambda l, j: (l, 0, j)),
        ],
        out_specs=pl.BlockSpec((1, rows, tn), lambda l, j: (l, 0, j)),
        out_shape=jax.ShapeDtypeStruct((DEPTH, rows, n), F32),
        compiler_params=_params("arbitrary", "arbitrary"),
        name="modulation",
    )(c_all, w_mod, b_mod.reshape(DEPTH, 1, n))


def _fused_even_kernel(h_ref, wu_ref, wva_ref, wza_ref, wq_ref, wk_ref, wv_ref, wz_ref,
                       cos_ref, sin_ref, wm_ref, bs_ref, lng_ref, retg_ref, *rest, tm, rc, aliased):
    rest = rest[1:] if aliased else rest
    ya_ref, yb_ref, sfin_ref, s_scr, dec_scr, qd_scr, kd_scr = rest
    b = pl.program_id(0)
    j = pl.program_id(1)
    i = pl.program_id(2)

    @pl.when((b == 0) & (j == 0) & (i == 0))
    def _():
        _fill_retention_tables(dec_scr, qd_scr, kd_scr, rc)

    n_chunks = tm // rc

    def projection(c, w_refs):
        rows = pl.ds(pl.multiple_of(i * tm + c * rc, rc), rc)
        p = [None] * len(w_refs)

        def piece(n):
            def run():
                p[n] = _dot(h_ref[rows, :], w_refs[n][0])
            return run

        return p, [piece(n) for n in range(len(w_refs))]

    @pl.when(j < A_HEADS)
    def _():
        w_refs = (wva_ref, wu_ref, wza_ref)
        cur, pieces = projection(0, w_refs)
        while pieces:
            _run_next(pieces)
        for c in range(n_chunks):
            nxt, pieces = projection(c + 1, w_refs) if c + 1 < n_chunks else (None, [])
            y, _ = _gmlp_unit(cur[1], cur[0], cur[2], lng_ref[0], wm_ref[0, 0], bs_ref[0, 0], GMLP_CHUNK, pieces)
            ya_ref[c * rc:(c + 1) * rc, :] = y.astype(ya_ref.dtype)
            cur = nxt

    @pl.when((j >= A_HEADS) & (i == 0))
    def _():
        s_scr[...] = jnp.zeros_like(s_scr)

    @pl.when(j >= A_HEADS)
    def _():
        hd = j - A_HEADS
        s = s_scr[...]
        dec = dec_scr[hd]
        qd = qd_scr[hd]
        kd = kd_scr[hd]
        w_refs = (wq_ref, wk_ref, wv_ref, wz_ref)
        cur, pieces = projection(0, w_refs)
        while pieces:
            _run_next(pieces)
        for c in range(n_chunks):
            rows = slice(c * rc, (c + 1) * rc)
            nxt, pieces = projection(c + 1, w_refs) if c + 1 < n_chunks else (None, [])
            y, s = _retention_chunk(cur[0], cur[1], cur[2], cur[3], s, cos_ref[rows, :], sin_ref[rows, :],
                                    dec, qd, kd, retg_ref[0], pieces)
            yb_ref[rows, :] = y.astype(yb_ref.dtype)
            cur = nxt
        s_scr[...] = s
        sfin_ref[0, 0, 0] = s
        _zero_other_slots(sfin_ref)


def _fused_even(h, w_in, cos, sin, wm, bs, ln_g3, ret_g3, prev_state, *, layer, batch, length):
    li = layer // 2
    tm, rc = PROMPT_TM, PROMPT_RC
    tpb = length // tm
    t = batch * length
    n_items = A_HEADS + B_HEADS

    def a_of(j):
        return jnp.minimum(j, A_HEADS - 1)

    def h_of(j):
        return jnp.maximum(j - A_HEADS, 0)

    def ya_row(b, j, i):
        return b * tpb + jnp.where(j < A_HEADS, i, tpb - 1)

    def yb_row(b, j, i):
        return b * tpb + jnp.where(j >= A_HEADS, i, 0)

    def wa_spec(seg):
        return pl.BlockSpec((1, D_MODEL, A_HEAD_DIM), lambda b, j, i: (li, 0, seg * A_HEADS + a_of(j)))

    def wb_spec(seg):
        base = 3 * A_WIDTH // B_DK + seg * B_HEADS
        return pl.BlockSpec((1, D_MODEL, B_DK), lambda b, j, i: (li, 0, base + h_of(j)))

    rope_mode = pl.Buffered(1) if tpb == 1 else None
    in_specs = [
        pl.BlockSpec((length, D_MODEL), lambda b, j, i: (b, 0)),
        wa_spec(0), wa_spec(1), wa_spec(2),
        wb_spec(0), wb_spec(1), wb_spec(2), wb_spec(3),
        pl.BlockSpec((tm, B_DK // 2), lambda b, j, i: (i, 0), pipeline_mode=rope_mode),
        pl.BlockSpec((tm, B_DK // 2), lambda b, j, i: (i, 0), pipeline_mode=rope_mode),
        pl.BlockSpec((1, 1, GMLP_CHUNK, GMLP_CHUNK), lambda b, j, i: (li, a_of(j), 0, 0)),
        pl.BlockSpec((1, 1, GMLP_CHUNK, A_HEAD_DIM), lambda b, j, i: (li, a_of(j), 0, 0)),
        pl.BlockSpec((1, 1, A_HEAD_DIM), lambda b, j, i: (li, 0, a_of(j))),
        pl.BlockSpec((1, 1, B_DV), lambda b, j, i: (li, 0, 0)),
    ]
    args = [h, w_in, w_in, w_in, w_in, w_in, w_in, w_in, cos, sin, wm, bs, ln_g3, ret_g3]
    aliases = _alias_prev(in_specs, args, prev_state, 2)
    n_state = DEPTH // 2
    return pl.pallas_call(
        functools.partial(_fused_even_kernel, tm=tm, rc=rc, aliased=prev_state is not None),
        grid=(batch, n_items, tpb),
        in_specs=in_specs,
        out_specs=[
            pl.BlockSpec((tm, A_HEAD_DIM), lambda b, j, i: (ya_row(b, j, i), a_of(j))),
            pl.BlockSpec((tm, B_DV), lambda b, j, i: (yb_row(b, j, i), h_of(j))),
            _stacked_out_spec(n_state, li, prev_state is None, (1, 1, B_DK, B_DV),
                              lambda b, j, i: (b, h_of(j), 0, 0)),
        ],
        out_shape=[
            jax.ShapeDtypeStruct((t, A_WIDTH), BF16),
            jax.ShapeDtypeStruct((t, B_WIDTH), BF16),
            jax.ShapeDtypeStruct((n_state, batch, B_HEADS, B_DK, B_DV), F32),
        ],
        scratch_shapes=[
            pltpu.VMEM((B_DK, B_DV), F32),
            pltpu.VMEM((B_HEADS, rc, rc), F32),
            pltpu.VMEM((B_HEADS, rc, B_DK // 2), F32),
            pltpu.VMEM((B_HEADS, rc, B_DK // 2), F32),
        ],
        input_output_aliases=aliases,
        compiler_params=_params("arbitrary", "arbitrary", "arbitrary"),
        name="fused_even",
    )(*args)


def _fused_odd_kernel(h_ref, wq_ref, wk_ref, wv_ref, wz_ref, wlr_ref, wgk_ref, bgk_ref,
                      glag_ref, *rest, tm, rc, blk, aliased):
    rest = rest[1:] if aliased else rest
    y_ref, sfin_ref, lr_scr, st_scr = rest
    hd = pl.program_id(1)
    i = pl.program_id(2)

    @pl.when(hd == 0)
    def _():
        lr_scr[i] = _dot(h_ref[pl.ds(pl.multiple_of(i * tm, tm), tm), :], wlr_ref[0]).astype(BF16)

    @pl.when(i == 0)
    def _():
        st_scr[...] = jnp.zeros_like(st_scr)

    def projection(c):
        rows = slice(c * rc, (c + 1) * rc)
        grows = pl.ds(pl.multiple_of(i * tm + c * rc, rc), rc)
        p = {}

        def piece_q():
            p["q"] = _dot(h_ref[grows, :], wq_ref[0]) * (C_DK ** -0.5)
            p["gk"] = _log_sigmoid(_dot(lr_scr[i, rows, :], wgk_ref[0]) + bgk_ref[0]) * (1.0 / GLA_GATE_NORM)

        def piece_k():
            p["k"] = _dot(h_ref[grows, :], wk_ref[0])

        def piece_v():
            p["v"] = _dot(h_ref[grows, :], wv_ref[0]).astype(BF16)

        def piece_z():
            p["z"] = _dot(h_ref[grows, :], wz_ref[0])

        return p, [piece_q, piece_k, piece_v, piece_z]

    tril = _block_tril(rc, rc)
    st = st_scr[...]
    n_chunks = tm // rc
    cur, pieces = projection(0)
    while pieces:
        _run_next(pieces)
    for c in range(n_chunks):
        nxt, pieces = projection(c + 1) if c + 1 < n_chunks else (None, [])
        y, st = _gla_chunk_anchored(cur["q"], cur["k"], cur["v"], cur["z"], cur["gk"], tril, st, glag_ref[0], blk,
                                    pieces)
        y_ref[c * rc:(c + 1) * rc, :] = y.astype(y_ref.dtype)
        cur = nxt
    st_scr[...] = st
    sfin_ref[0, 0, 0] = st.T
    _zero_other_slots(sfin_ref)


def _fused_odd(h, w_in, w_lr, w_gk, b_gk3, gla_g3, prev_state, *, layer, batch, length):
    li = layer // 2
    tm, rc, blk = PROMPT_TM, PROMPT_RC, PROMPT_GLA_BLK
    tpb = length // tm
    t = batch * length
    v_base = 2 * C_HEADS * C_DK // C_DV

    in_specs = [
        pl.BlockSpec((length, D_MODEL), lambda b, h, i: (b, 0)),
        pl.BlockSpec((1, D_MODEL, C_DK), lambda b, h, i: (li, 0, h)),
        pl.BlockSpec((1, D_MODEL, C_DK), lambda b, h, i: (li, 0, C_HEADS + h)),
        pl.BlockSpec((1, D_MODEL, C_DV), lambda b, h, i: (li, 0, v_base + h)),
        pl.BlockSpec((1, D_MODEL, C_DV), lambda b, h, i: (li, 0, v_base + C_HEADS + h)),
        pl.BlockSpec((1, D_MODEL, LR_PAD), lambda b, h, i: (li, 0, 0)),
        pl.BlockSpec((1, LR_PAD, C_DK), lambda b, h, i: (li, 0, h)),
        pl.BlockSpec((1, 1, C_DK), lambda b, h, i: (li, 0, h)),
        pl.BlockSpec((1, 1, C_DV), lambda b, h, i: (li, 0, 0)),
    ]
    args = [h, w_in, w_in, w_in, w_in, w_lr, w_gk, b_gk3, gla_g3]
    aliases = _alias_prev(in_specs, args, prev_state, 1)
    n_state = DEPTH // 2
    return pl.pallas_call(
        functools.partial(_fused_odd_kernel, tm=tm, rc=rc, blk=blk, aliased=prev_state is not None),
        grid=(batch, C_HEADS, tpb),
        in_specs=in_specs,
        out_specs=[
            pl.BlockSpec((tm, C_DV), lambda b, h, i: (b * tpb + i, h)),
            _stacked_out_spec(n_state, li, prev_state is None, (1, 1, C_DK, C_DV), lambda b, h, i: (b, h, 0, 0)),
        ],
        out_shape=[
            jax.ShapeDtypeStruct((t, C_WIDTH), BF16),
            jax.ShapeDtypeStruct((n_state, batch, C_HEADS, C_DK, C_DV), F32),
        ],
        scratch_shapes=[
            pltpu.VMEM((tpb, tm, LR_PAD), BF16),
            pltpu.VMEM((C_DV, C_DK), F32),
        ],
        input_output_aliases=aliases,
        compiler_params=_params("arbitrary", "arbitrary", "arbitrary"),
        name="fused_odd",
    )(*args)


def _first_norm_kernel(x_ref, sh_ref, sc_ref, g_ref, h_ref):
    h_ref[...] = _norm_modulate(x_ref[...], g_ref[0], sc_ref[...], sh_ref[...])


def _first_norm(x, mod5, norm_g3, *, length):
    t, d = x.shape
    tm = OUT_NORM_TM
    tpb = length // tm

    def mod_spec(which):
        return pl.BlockSpec((None, None, None, 1, d), lambda i: (0, i // tpb, which, 0, 0))

    return pl.pallas_call(
        _first_norm_kernel,
        grid=(t // tm,),
        in_specs=[pl.BlockSpec((tm, d), lambda i: (i, 0)), mod_spec(0), mod_spec(1),
                  pl.BlockSpec((1, 1, d), lambda i: (0, 0, 0))],
        out_specs=pl.BlockSpec((tm, d), lambda i: (i, 0)),
        out_shape=jax.ShapeDtypeStruct((t, d), BF16),
        compiler_params=_params("arbitrary"),
        name="first_norm",
    )(x, mod5, mod5, norm_g3)


def _out_norm_kernel(*refs, n_parts, n_chunks, last):
    y_refs = refs[:n_parts]
    w_refs = refs[n_parts:2 * n_parts]
    if last:
        x_ref, gate_ref, g_ref, o_ref, xn_ref = refs[2 * n_parts:]
    else:
        x_ref, gate_ref, g_ref, sh_ref, sc_ref, xn_ref, h_ref = refs[2 * n_parts:]
    d = x_ref.shape[-1]
    tn = d // n_chunks

    def residual(n, acc):
        cols = slice(n * tn, (n + 1) * tn)
        xn = x_ref[:, cols] + gate_ref[:, cols] * acc
        xn_ref[:, cols] = xn
        return jnp.sum(xn * xn, axis=-1, keepdims=True)

    ssq = 0.0
    pending = None
    for n in range(n_chunks):
        cols = slice(n * tn, (n + 1) * tn)
        acc = _dot(y_refs[0][...], w_refs[0][0, :, cols])
        for p in range(1, n_parts):
            acc = acc + _dot(y_refs[p][...], w_refs[p][0, :, cols])
        if pending is not None:
            ssq = ssq + residual(*pending)
        pending = (n, acc)
    ssq = ssq + residual(*pending)
    r = lax.rsqrt(ssq * (1.0 / d) + EPS)
    for n in range(n_chunks):
        cols = slice(n * tn, (n + 1) * tn)
        y = xn_ref[:, cols] * r * g_ref[0][:, cols]
        if last:
            o_ref[:, cols] = y
        else:
            h_ref[:, cols] = (y * (1.0 + sc_ref[:, cols]) + sh_ref[:, cols]).astype(h_ref.dtype)


def _out_norm(y_parts, w, x, mod5, norm_g3, final_g, *, layer, length):
    li = layer // 2
    last = layer == DEPTH - 1
    t, d = x.shape
    tm = OUT_NORM_TM
    tpb = length // tm
    n_parts = len(y_parts)
    kp = w.shape[1] // n_parts

    def mod_spec(lyr, which):
        return pl.BlockSpec((None, None, None, 1, d), lambda i: (lyr, i // tpb, which, 0, 0))

    row_spec = pl.BlockSpec((tm, d), lambda i: (i, 0))
    in_specs = [pl.BlockSpec((tm, kp), lambda i: (i, 0)) for _ in range(n_parts)]
    in_specs += [pl.BlockSpec((1, kp, d), functools.partial(lambda i, p: (li, p, 0), p=p),
                              pipeline_mode=pl.Buffered(1)) for p in range(n_parts)]
    in_specs += [row_spec, mod_spec(layer, 2)]
    args = [*y_parts, *([w] * n_parts), x, mod5]
    if last:
        in_specs.append(pl.BlockSpec((1, 1, d), lambda i: (0, 0, 0)))
        args.append(final_g.reshape(1, 1, d))
        out_specs = [row_spec]
        out_shape = [jax.ShapeDtypeStruct((t, d), F32)]
        scratch = [pltpu.VMEM((tm, d), F32)]
    else:
        in_specs += [pl.BlockSpec((1, 1, d), lambda i: (layer + 1, 0, 0)),
                     mod_spec(layer + 1, 0), mod_spec(layer + 1, 1)]
        args += [norm_g3, mod5, mod5]
        out_specs = [row_spec, row_spec]
        out_shape = [jax.ShapeDtypeStruct((t, d), F32), jax.ShapeDtypeStruct((t, d), BF16)]
        scratch = []
    return pl.pallas_call(
        functools.partial(_out_norm_kernel, n_parts=n_parts, n_chunks=d // OUT_TN, last=last),
        grid=(t // tm,),
        in_specs=in_specs,
        out_specs=out_specs,
        out_shape=out_shape,
        scratch_shapes=scratch,
        compiler_params=_params("arbitrary"),
        name="out_norm",
    )(*args)


def _out_proj_kernel(*refs, n_parts):
    y_refs = refs[:n_parts]
    w_refs = refs[n_parts:2 * n_parts]
    x_ref, gate_ref, o_ref = refs[2 * n_parts:]
    acc = _dot(y_refs[0][...], w_refs[0][0])
    for p in range(1, n_parts):
        acc = acc + _dot(y_refs[p][...], w_refs[p][0])
    o_ref[...] = x_ref[...] + gate_ref[...] * acc


def _out_proj(y_parts, w, x, gate_arr, gate_spec, *, layer, tm):
    li = layer // 2
    t, d = x.shape
    tn = OUT_TN
    n_parts = len(y_parts)
    kp = w.shape[1] // n_parts
    in_specs = [pl.BlockSpec((tm, kp), lambda i, j: (i, 0)) for _ in range(n_parts)]
    in_specs += [pl.BlockSpec((1, kp, tn), functools.partial(lambda i, j, p: (li, p, j), p=p)) for p in range(n_parts)]
    in_specs += [pl.BlockSpec((tm, tn), lambda i, j: (i, j)), gate_spec]
    return pl.pallas_call(
        functools.partial(_out_proj_kernel, n_parts=n_parts),
        grid=(t // tm, d // tn),
        in_specs=in_specs,
        out_specs=pl.BlockSpec((tm, tn), lambda i, j: (i, j)),
        out_shape=jax.ShapeDtypeStruct((t, d), F32),
        compiler_params=_params("arbitrary", "arbitrary"),
        name="out_proj",
    )(*y_parts, *([w] * n_parts), x, gate_arr)


def _final_norm_kernel(x_ref, g_ref, o_ref):
    o_ref[...] = _rms(x_ref[...], g_ref[...])


def _final_norm(x, g, *, tm):
    t, d = x.shape
    return pl.pallas_call(
        _final_norm_kernel,
        grid=(t // tm,),
        in_specs=[pl.BlockSpec((tm, d), lambda i: (i, 0)), pl.BlockSpec((1, d), lambda i: (0, 0))],
        out_specs=pl.BlockSpec((tm, d), lambda i: (i, 0)),
        out_shape=jax.ShapeDtypeStruct((t, d), F32),
        compiler_params=_params("arbitrary"),
        name="final_norm",
    )(x, g)


def _in_proj_kernel(x_ref, sh_ref, sc_ref, g_ref, w_ref, *rest, has_lr):
    if has_lr:
        wlr_ref, o_ref, lr_ref, h_scr = rest
    else:
        o_ref, h_scr = rest

    @pl.when(pl.program_id(0) == 0)
    def _():
        h = _norm_modulate(x_ref[...], g_ref[0], sc_ref[...], sh_ref[...])
        h_scr[...] = h
        if has_lr:
            lr_ref[...] = _dot(h, wlr_ref[0]).astype(BF16)

    o_ref[...] = _dot(h_scr[...], w_ref[0]).astype(o_ref.dtype)


def _in_proj(x, mod_rows, norm_g3, w, wlr, *, layer):
    li = layer // 2
    t, d = x.shape
    tn = PROJ_TN
    n = w.shape[2] // tn * tn
    has_lr = wlr is not None
    in_specs = [
        pl.BlockSpec((t, d), lambda j: (0, 0)),
        pl.BlockSpec((None, None, t, d), lambda j: (layer, 0, 0, 0)),
        pl.BlockSpec((None, None, t, d), lambda j: (layer, 1, 0, 0)),
        pl.BlockSpec((1, 1, d), lambda j: (layer, 0, 0)),
        pl.BlockSpec((1, d, tn), lambda j: (li, 0, j)),
    ]
    args = [x, mod_rows, mod_rows, norm_g3, w]
    out_specs = [pl.BlockSpec((t, tn), lambda j: (0, j))]
    out_shape = [jax.ShapeDtypeStruct((t, n), BF16)]
    if has_lr:
        in_specs.append(pl.BlockSpec((1, d, LR_PAD), lambda j: (li, 0, 0)))
        args.append(wlr)
        out_specs.append(pl.BlockSpec((t, LR_PAD), lambda j: (0, 0)))
        out_shape.append(jax.ShapeDtypeStruct((t, LR_PAD), BF16))
    return pl.pallas_call(
        functools.partial(_in_proj_kernel, has_lr=has_lr),
        grid=(n // tn,),
        in_specs=in_specs,
        out_specs=out_specs,
        out_shape=out_shape,
        scratch_shapes=[pltpu.VMEM((t, d), BF16)],
        compiler_params=_params("arbitrary"),
        name="in_proj",
    )(*args)


def _even_mix_kernel(p_ref, cos_ref, sin_ref, wm_ref, bs_ref, lng_ref, retg_ref, s0_ref, *rest, ts, aliased):
    rest = rest[2:] if aliased else rest
    y_ref, sfin_ref, va_ref, dec_scr, qd_scr, kd_scr = rest

    @pl.when(pl.program_id(0) == 0)
    def _():
        _fill_retention_tables(dec_scr, qd_scr, kd_scr, ts)

    _zero_other_slots(sfin_ref)
    _zero_other_slots(va_ref)
    for a in range(A_HEADS):
        c0 = a * A_HEAD_DIM
        u = p_ref[:, c0:c0 + A_HEAD_DIM].astype(F32)
        v = p_ref[:, A_WIDTH + c0:A_WIDTH + c0 + A_HEAD_DIM].astype(F32)
        z = p_ref[:, 2 * A_WIDTH + c0:2 * A_WIDTH + c0 + A_HEAD_DIM].astype(F32)
        y, vn = _gmlp_unit(u, v, z, lng_ref[0, :, c0:c0 + A_HEAD_DIM], wm_ref[0, a], bs_ref[0, a], ts)
        va_ref[0, :, c0:c0 + A_HEAD_DIM] = vn
        y_ref[:, c0:c0 + A_HEAD_DIM] = y.astype(y_ref.dtype)

    q0 = 3 * A_WIDTH
    k0 = q0 + B_HEADS * B_DK
    v0 = k0 + B_HEADS * B_DK
    z0 = v0 + B_WIDTH
    cos = cos_ref[...]
    sin = sin_ref[...]
    for h in range(B_HEADS):
        hc = h * B_DK
        y, s_new = _retention_chunk(
            p_ref[:, q0 + hc:q0 + hc + B_DK].astype(F32), p_ref[:, k0 + hc:k0 + hc + B_DK].astype(F32),
            p_ref[:, v0 + hc:v0 + hc + B_DV].astype(F32), p_ref[:, z0 + hc:z0 + hc + B_DV].astype(F32),
            s0_ref[0, 0, h], cos, sin, dec_scr[h], qd_scr[h], kd_scr[h], retg_ref[0])
        sfin_ref[0, 0, h] = s_new
        y_ref[:, A_WIDTH + hc:A_WIDTH + hc + B_DV] = y.astype(y_ref.dtype)


def _even_mix(p, cos, sin, wm, bs, ln_g3, ret_g3, state, prev_state, prev_va, *, layer, batch, ts):
    li = layer // 2
    n_state = DEPTH // 2
    in_specs = [
        pl.BlockSpec((ts, E_IN), lambda b: (b, 0)),
        pl.BlockSpec((ts, B_DK // 2), lambda b: (0, 0)),
        pl.BlockSpec((ts, B_DK // 2), lambda b: (0, 0)),
        pl.BlockSpec((1, A_HEADS, ts, ts), lambda b: (li, 0, 0, 0)),
        pl.BlockSpec((1, A_HEADS, ts, A_HEAD_DIM), lambda b: (li, 0, 0, 0)),
        pl.BlockSpec((1, 1, A_WIDTH), lambda b: (li, 0, 0)),
        pl.BlockSpec((1, 1, B_DV), lambda b: (li, 0, 0)),
        pl.BlockSpec((1, 1, B_HEADS, B_DK, B_DV), lambda b: (li, b, 0, 0, 0)),
    ]
    args = [p, cos, sin, wm, bs, ln_g3, ret_g3, state]
    aliases = _alias_prev(in_specs, args, prev_state, 1)
    aliases.update(_alias_prev(in_specs, args, prev_va, 2))
    return pl.pallas_call(
        functools.partial(_even_mix_kernel, ts=ts, aliased=prev_state is not None),
        grid=(batch,),
        in_specs=in_specs,
        out_specs=[
            pl.BlockSpec((ts, E_MIX), lambda b: (b, 0)),
            _stacked_out_spec(n_state, li, prev_state is None, (1, B_HEADS, B_DK, B_DV), lambda b: (b, 0, 0, 0)),
            _stacked_out_spec(n_state, li, prev_va is None, (ts, A_WIDTH), lambda b: (b, 0)),
        ],
        out_shape=[
            jax.ShapeDtypeStruct((batch * ts, E_MIX), BF16),
            jax.ShapeDtypeStruct((n_state, batch, B_HEADS, B_DK, B_DV), F32),
            jax.ShapeDtypeStruct((n_state, batch * ts, A_WIDTH), F32),
        ],
        scratch_shapes=[
            pltpu.VMEM((B_HEADS, ts, ts), F32),
            pltpu.VMEM((B_HEADS, ts, B_DK // 2), F32),
            pltpu.VMEM((B_HEADS, ts, B_DK // 2), F32),
        ],
        input_output_aliases=aliases,
        compiler_params=_params("arbitrary"),
        name="even_mix",
    )(*args)


def _odd_mix_kernel(p_ref, lr_ref, wgk_ref, bgk_ref, glag_ref, s0_ref, *rest, ts, aliased):
    rest = rest[1:] if aliased else rest
    y_ref, sfin_ref = rest
    _zero_other_slots(sfin_ref)
    gk = _log_sigmoid(_dot(lr_ref[...], wgk_ref[0]) + bgk_ref[0]) * (1.0 / GLA_GATE_NORM)
    tril = _block_tril(ts, ts)
    causal = _causal(ts)
    k0 = C_HEADS * C_DK
    v0 = 2 * C_HEADS * C_DK
    z0 = v0 + C_WIDTH
    for h in range(C_HEADS):
        kc = h * C_DK
        vc = h * C_DV
        bcum = _cumsum_rows(gk[:, kc:kc + C_DK], tril)
        q = p_ref[:, kc:kc + C_DK].astype(F32) * (C_DK ** -0.5)
        k = p_ref[:, k0 + kc:k0 + kc + C_DK].astype(F32)
        v = p_ref[:, v0 + vc:v0 + vc + C_DV]
        z = p_ref[:, z0 + vc:z0 + vc + C_DV].astype(F32)
        y, st = _gla_chunk(q, k, v, z, bcum, s0_ref[0, 0, h].T, causal, glag_ref[0])
        sfin_ref[0, 0, h] = st.T
        y_ref[:, vc:vc + C_DV] = y.astype(y_ref.dtype)


def _odd_mix(p, lr, w_gk, b_gk3, gla_g3, state, prev_state, *, layer, batch, ts):
    li = layer // 2
    n_state = DEPTH // 2
    in_specs = [
        pl.BlockSpec((ts, O_MAIN), lambda b: (b, 0)),
        pl.BlockSpec((ts, LR_PAD), lambda b: (b, 0)),
        pl.BlockSpec((1, LR_PAD, C_HEADS * C_DK), lambda b: (li, 0, 0)),
        pl.BlockSpec((1, 1, C_HEADS * C_DK), lambda b: (li, 0, 0)),
        pl.BlockSpec((1, 1, C_DV), lambda b: (li, 0, 0)),
        pl.BlockSpec((1, 1, C_HEADS, C_DK, C_DV), lambda b: (li, b, 0, 0, 0)),
    ]
    args = [p, lr, w_gk, b_gk3, gla_g3, state]
    aliases = _alias_prev(in_specs, args, prev_state, 1)
    return pl.pallas_call(
        functools.partial(_odd_mix_kernel, ts=ts, aliased=prev_state is not None),
        grid=(batch,),
        in_specs=in_specs,
        out_specs=[
            pl.BlockSpec((ts, C_WIDTH), lambda b: (b, 0)),
            _stacked_out_spec(n_state, li, prev_state is None, (1, C_HEADS, C_DK, C_DV), lambda b: (b, 0, 0, 0)),
        ],
        out_shape=[
            jax.ShapeDtypeStruct((batch * ts, C_WIDTH), BF16),
            jax.ShapeDtypeStruct((n_state, batch, C_HEADS, C_DK, C_DV), F32),
        ],
        input_output_aliases=aliases,
        compiler_params=_params("arbitrary"),
        name="odd_mix",
    )(*args)


def _rope_tables(pos):
    half = B_DK // 2
    freqs = ROPE_BASE ** (-jnp.arange(half, dtype=F32) / half)
    ang = pos.astype(F32)[:, None] * freqs[None, :]
    return jnp.cos(ang), jnp.sin(ang)


def _gmlp_weights(gmlp_ws, gmlp_bs, seg):
    p = np.arange(seg)
    mask = (p[None, :] // CHUNK) <= (p[:, None] // CHUNK)
    wm = jnp.where(mask[None, None], gmlp_ws[:, :, :seg, :seg], 0.0).astype(BF16)
    bs = jnp.broadcast_to(gmlp_bs[:, :, :seg, None], gmlp_bs.shape[:2] + (seg, A_HEAD_DIM))
    return wm, bs


def _prompt_trunk(x, mod5, w, *, batch, length):
    cos, sin = _rope_tables(jnp.arange(length))
    wm, bs = _gmlp_weights(w["gmlp_ws"], w["gmlp_bs"], GMLP_CHUNK)
    ret, gla = None, None
    h = _first_norm(x, mod5, w["norm_g"], length=length)
    for layer in range(DEPTH):
        if layer % 2 == 0:
            ya, yb, ret = _fused_even(h, w["w_in_e"], cos, sin, wm, bs, w["gmlp_ln_g"], w["ret_norm_g"], ret,
                                      layer=layer, batch=batch, length=length)
            outs = _out_norm([ya, yb], w["w_out_e"], x, mod5, w["norm_g"], w["final_g"], layer=layer, length=length)
        else:
            y, gla = _fused_odd(h, w["w_in_o"], w["w_lr"], w["w_gk"], w["b_gk"], w["gla_norm_g"], gla,
                                layer=layer, batch=batch, length=length)
            outs = _out_norm([y], w["w_out_o"], x, mod5, w["norm_g"], w["final_g"], layer=layer, length=length)
        if layer < DEPTH - 1:
            x, h = outs
    return outs[0], ret, gla


def _sample_trunk(x, mod_rows, state_ret, state_gla, w, *, batch, length):
    t = batch * length
    cos, sin = _rope_tables(PAST_LEN + jnp.arange(length))
    wm, bs = _gmlp_weights(w["gmlp_ws"], w["gmlp_bs"], length)
    gate_spec = lambda layer: pl.BlockSpec((None, None, t, OUT_TN), lambda i, j: (layer, 2, 0, j))
    ret, gla, va = None, None, None
    for layer in range(DEPTH):
        if layer % 2 == 0:
            (p,) = _in_proj(x, mod_rows, w["norm_g"], w["w_in_e"], None, layer=layer)
            y, ret, va = _even_mix(p, cos, sin, wm, bs, w["gmlp_ln_g"], w["ret_norm_g"], state_ret, ret, va,
                                   layer=layer, batch=batch, ts=length)
            x = _out_proj([y], w["w_out_e"], x, mod_rows, gate_spec(layer), layer=layer, tm=t)
        else:
            p, lr = _in_proj(x, mod_rows, w["norm_g"], w["w_in_o"], w["w_lr"], layer=layer)
            y, gla = _odd_mix(p, lr, w["w_gk"], w["b_gk"], w["gla_norm_g"], state_gla, gla,
                              layer=layer, batch=batch, ts=length)
            x = _out_proj([y], w["w_out_o"], x, mod_rows, gate_spec(layer), layer=layer, tm=t)
    return _final_norm(x, w["final_g"], tm=t), ret, gla, va


def kernel(x_prompt, x_sample, state_ret, state_gla, c_prompt, c_sample, norm_g, w_mod, b_mod, w_in_e, gmlp_ln_g,
           gmlp_ws, gmlp_bs, ret_norm_g, w_out_e, w_in_o, w_gk2, b_gk, gla_norm_g, w_out_o, final_g):
    bp, lp, d = x_prompt.shape
    bs_, ls = x_sample.shape[:2]
    n_even, n_odd = w_in_e.shape[0], w_in_o.shape[0]

    w = {
        "norm_g": norm_g.reshape(DEPTH, 1, d),
        "w_in_e": w_in_e.astype(BF16),
        "w_out_e": w_out_e.astype(BF16),
        "w_in_o": w_in_o.astype(BF16),
        "w_lr": jnp.pad(w_in_o[:, :, O_MAIN:], ((0, 0), (0, 0), (0, LR_PAD - GLA_RANK))).astype(BF16),
        "w_gk": jnp.pad(w_gk2, ((0, 0), (0, LR_PAD - GLA_RANK), (0, 0))).astype(BF16),
        "w_out_o": w_out_o.astype(BF16),
        "gmlp_ws": gmlp_ws, "gmlp_bs": gmlp_bs,
        "gmlp_ln_g": gmlp_ln_g.reshape(n_even, 1, A_WIDTH),
        "ret_norm_g": ret_norm_g.reshape(n_even, 1, B_DV),
        "b_gk": b_gk.reshape(n_odd, 1, C_HEADS * C_DK),
        "gla_norm_g": gla_norm_g.reshape(n_odd, 1, C_DV),
        "final_g": final_g.reshape(1, d),
    }

    c_all = jnp.concatenate([c_prompt, c_sample], axis=0)
    mod = _modulation(c_all, w_mod, b_mod)
    mod5 = mod.reshape(DEPTH, bp + bs_, 3, 1, d)
    mod_rows = jnp.repeat(mod[:, bp:].reshape(DEPTH, bs_, 3, d).transpose(0, 2, 1, 3), ls, axis=2)

    y_p, ret_p, gla_p = _prompt_trunk(x_prompt.reshape(bp * lp, d), mod5, w, batch=bp, length=lp)
    y_s, ret_s, gla_s, va = _sample_trunk(x_sample.reshape(bs_ * ls, d), mod_rows, state_ret, state_gla, w,
                                          batch=bs_, length=ls)
    return (y_p.reshape(bp, lp, d), y_s.reshape(bs_, ls, d), ret_p, ret_s, gla_p, gla_s,
            va.reshape(n_even, bs_, ls, A_WIDTH))
```

```python
import functools
import math

import jax
import jax.numpy as jnp
import numpy as np
from jax import lax
from jax.experimental import pallas as pl
from jax.experimental.pallas import tpu as pltpu

F32 = jnp.float32
BF16 = jnp.bfloat16

D_MODEL = 2048
DEPTH = 4
PAST_LEN = 1024
EPS = 1e-6
CHUNK = 64
A_HEADS = 4
A_HEAD_DIM = 512
A_WIDTH = 2048
GMLP_CHUNK = 128
B_HEADS = 8
B_DK = 256
B_DV = 256
B_WIDTH = 2048
ROPE_BASE = 10000.0
C_HEADS = 8
C_DK = 256
C_DV = 512
C_WIDTH = 4096
GLA_RANK = 16
GLA_GATE_NORM = 16.0
E_IN = 3 * A_WIDTH + 2 * B_HEADS * B_DK + 2 * B_WIDTH
E_MIX = A_WIDTH + B_WIDTH
O_MAIN = 2 * C_HEADS * C_DK + 2 * C_WIDTH
LR_PAD = 128

VMEM_LIMIT_BYTES = 56 * 1024 * 1024

PROMPT_TM = 2048
PROMPT_RC = 256
PROMPT_GLA_BLK = 16
PROJ_TN = 1024
OUT_TN = 512
OUT_NORM_TM = 512
OUT_NORM_ROWS = 256


def _params(*sem):
    return pltpu.CompilerParams(dimension_semantics=sem, vmem_limit_bytes=VMEM_LIMIT_BYTES)


def _dot(a, b):
    return jnp.dot(a, b, preferred_element_type=F32)


def _dot_nt(a, b):
    return lax.dot_general(a, b, (((1,), (1,)), ((), ())), preferred_element_type=F32)


def _dot_tn(a, b):
    return lax.dot_general(a, b, (((0,), (0,)), ((), ())), preferred_element_type=F32)


def _silu(x):
    return x * jax.nn.sigmoid(x)


def _log_sigmoid(x):
    return jnp.minimum(x, 0.0) - jnp.log1p(jnp.exp(-jnp.abs(x)))


def _rms(x, g):
    return x * lax.rsqrt(jnp.mean(x * x, axis=-1, keepdims=True) + EPS) * g


def _norm_modulate(x, g, scale, shift):
    return (_rms(x, g) * (1.0 + scale) + shift).astype(BF16)


def _retention_log_gamma():
    return [math.log1p(-(2.0 ** (-5.0 - h))) for h in range(B_HEADS)]


def _fill_retention_tables(dec_scr, qd_scr, kd_scr, n):
    log_g = _retention_log_gamma()
    row = lax.broadcasted_iota(jnp.int32, (n, n), 0).astype(F32)
    col = lax.broadcasted_iota(jnp.int32, (n, n), 1).astype(F32)
    diff = row - col
    rowd = lax.broadcasted_iota(jnp.int32, (n, B_DK // 2), 0).astype(F32)
    for h in range(B_HEADS):
        dec_scr[h] = jnp.where(diff >= 0, jnp.exp(jnp.maximum(diff, 0.0) * log_g[h]), 0.0)
        qd_scr[h] = jnp.exp((rowd + 1.0) * log_g[h])
        kd_scr[h] = jnp.exp((n - 1.0 - rowd) * log_g[h])


def _rotary(xh, cos, sin):
    half = xh.shape[-1] // 2
    x1 = xh[:, :half]
    x2 = xh[:, half:]
    return jnp.concatenate([x1 * cos - x2 * sin, x1 * sin + x2 * cos], axis=-1)


def _run_next(fillers):
    if fillers:
        fillers.pop(0)()


def _gmlp_unit(u, v, z, g_ln, wm, bias, seg, fillers=()):
    fillers = list(fillers)
    v = jax.nn.gelu(v)
    vc = v - jnp.mean(v, axis=-1, keepdims=True)
    vn = vc * lax.rsqrt(jnp.mean(vc * vc, axis=-1, keepdims=True) + EPS) * g_ln
    vb = vn.astype(BF16)
    _run_next(fillers)
    mixed = [_dot(wm, vb[s * seg:(s + 1) * seg]) + bias for s in range(u.shape[0] // seg)]
    mixed = mixed[0] if len(mixed) == 1 else jnp.concatenate(mixed, axis=0)
    _run_next(fillers)
    y = jax.nn.gelu(u) * mixed * _silu(z)
    while fillers:
        _run_next(fillers)
    return y, vn


def _retention_chunk(q, k, v, z, s_old, cos, sin, dec, qd, kd, g_ret, fillers=()):
    fillers = list(fillers)
    n = q.shape[0]
    qd = jnp.concatenate([qd, qd], axis=-1)
    kd = jnp.concatenate([kd, kd], axis=-1)
    q = _rotary(q, cos, sin)
    k = _rotary(k, cos, sin) * (B_DK ** -0.5)
    vb = v.astype(BF16)
    _run_next(fillers)
    scores = _dot_nt(q.astype(BF16), k.astype(BF16)) * dec
    s_new = qd[n - 1:n, :] * s_old + _dot_tn((k * kd).astype(BF16), vb)
    _run_next(fillers)
    o = _dot(scores.astype(BF16), vb) + _dot((q * qd).astype(BF16), s_old.astype(BF16))
    _run_next(fillers)
    y = _rms(o, g_ret) * _silu(z)
    while fillers:
        _run_next(fillers)
    return y, s_new


def _gla_chunk(q, k, v, z, bcum, st_old, causal, g_gla):
    n = q.shape[0]
    bend = bcum[n - 1:n, :]
    qf = (q * jnp.exp(bcum)).astype(BF16)
    kf = (k * jnp.exp(-bcum)).astype(BF16)
    scores = jnp.where(causal, _dot_nt(qf, kf), 0.0).astype(BF16)
    o = _dot(scores, v) + _dot_nt(qf, st_old.astype(BF16))
    k_tail = (k * jnp.exp(bend - bcum)).astype(BF16)
    st_new = st_old * jnp.exp(bend) + _dot_tn(v, k_tail)
    return _rms(o, g_gla) * _silu(z), st_new


def _gla_chunk_anchored(q, k, v, z, gk, tril, st_old, g_gla, fine, fillers):
    fillers = list(fillers)
    n = q.shape[0]
    win = 128
    bcum = _cumsum_rows(gk, tril)
    _run_next(fillers)
    bend = bcum[n - 1:n, :]
    score_rows = []
    for w0 in range(0, n, win):
        w1 = w0 + win
        kw, bw = k[w0:w1], bcum[w0:w1]
        row = lax.broadcasted_iota(jnp.int32, (win, C_DK), 0) + w0
        groups = []
        for lo in range(w0, w1, fine):
            hi = lo + fine
            anchor = bcum[lo - 1:lo, :] if lo else jnp.zeros_like(bend)
            qg = (q[lo:hi] * jnp.exp(bcum[lo:hi] - anchor)).astype(BF16)
            kg = (kw * jnp.exp(jnp.where(row < hi, anchor - bw, 0.0))).astype(BF16)
            t_idx = lax.broadcasted_iota(jnp.int32, (fine, win), 0) + lo
            s_idx = lax.broadcasted_iota(jnp.int32, (fine, win), 1) + w0
            groups.append(jnp.where(t_idx >= s_idx, _dot_nt(qg, kg), 0.0).astype(BF16))
        entries = [(jnp.concatenate(groups, axis=0), slice(w0, w1))]
        if w0:
            anchor = bcum[w0 - 1:w0, :]
            qw = (q[w0:w1] * jnp.exp(bw - anchor)).astype(BF16)
            kp = (k[:w0] * jnp.exp(anchor - bcum[:w0])).astype(BF16)
            entries.append((_dot_nt(qw, kp).astype(BF16), slice(0, w0)))
        score_rows.append(entries)
    q_state = (q * jnp.exp(bcum)).astype(BF16)
    k_tail = (k * jnp.exp(bend - bcum)).astype(BF16)
    _run_next(fillers)
    o_rows = []
    for entries in score_rows:
        acc = None
        for s, key_rows in entries:
            part = _dot(s, v[key_rows])
            acc = part if acc is None else acc + part
        o_rows.append(acc)
    o = jnp.concatenate(o_rows, axis=0) + _dot_nt(q_state, st_old.astype(BF16))
    st_new = st_old * jnp.exp(bend) + _dot_tn(v, k_tail)
    _run_next(fillers)
    y = _rms(o, g_gla) * _silu(z)
    while fillers:
        _run_next(fillers)
    return y, st_new


def _cumsum_rows(g, tril):
    g_hi = g.astype(BF16)
    g_lo = (g - g_hi.astype(F32)).astype(BF16)
    return _dot(tril, g_hi) + _dot(tril, g_lo)


def _block_tril(n, blk):
    row = lax.broadcasted_iota(jnp.int32, (n, n), 0)
    col = lax.broadcasted_iota(jnp.int32, (n, n), 1)
    return jnp.where((row >= col) & (row // blk == col // blk), 1.0, 0.0).astype(BF16)


def _causal(n):
    return lax.broadcasted_iota(jnp.int32, (n, n), 0) >= lax.broadcasted_iota(jnp.int32, (n, n), 1)


def _alias_prev(in_specs, args, prev, out_index):
    if prev is None:
        return {}
    in_specs.append(pl.BlockSpec(memory_space=pl.ANY))
    args.append(prev)
    return {len(args) - 1: out_index}


def _stacked_out_spec(n_stack, slot, first, tail_shape, tail_index):
    if first:
        return pl.BlockSpec((n_stack,) + tail_shape, lambda *g: (0,) + tail_index(*g))
    return pl.BlockSpec((1,) + tail_shape, lambda *g: (slot,) + tail_index(*g))


def _zero_other_slots(ref):
    if ref.shape[0] > 1:
        ref[1:] = jnp.zeros((ref.shape[0] - 1,) + ref.shape[1:], ref.dtype)


def _cast_kernel(w_ref, o_ref):
    o_ref[...] = w_ref[...].astype(o_ref.dtype)


def _cast_leading_columns(w, n_cols, *, rows_per_block):
    n_stack, rows, _ = w.shape
    spec = pl.BlockSpec((1, rows_per_block, n_cols), lambda s, r: (s, r, 0))
    return pl.pallas_call(
        _cast_kernel,
        grid=(n_stack, rows // rows_per_block),
        in_specs=[spec],
        out_specs=spec,
        out_shape=jax.ShapeDtypeStruct((n_stack, rows, n_cols), BF16),
        compiler_params=_params("arbitrary", "arbitrary"),
        name="cast_weights",
    )(w)


def _mod_kernel(c_ref, w_ref, b_ref, o_ref):
    s = _silu(c_ref[...]).astype(BF16)
    o_ref[0] = _dot(s, w_ref[0].astype(BF16)) + b_ref[0]


def _modulation(c_all, w_mod, b_mod):
    rows = c_all.shape[0]
    tn = 1024
    n = 3 * D_MODEL
    return pl.pallas_call(
        _mod_kernel,
        grid=(DEPTH, n // tn),
        in_specs=[
            pl.BlockSpec((rows, D_MODEL), lambda l, j: (0, 0)),
            pl.BlockSpec((1, D_MODEL, tn), lambda l, j: (l, 0, j)),
            pl.BlockSpec((1, 1, tn), lambda l, j: (l, 0, j)),
        ],
        out_specs=pl.BlockSpec((1, rows, tn), lambda l, j: (l, 0, j)),
        out_shape=jax.ShapeDtypeStruct((DEPTH, rows, n), F32),
        compiler_params=_params("arbitrary", "arbitrary"),
        name="modulation",
    )(c_all, w_mod, b_mod.reshape(DEPTH, 1, n))


def _fused_even_kernel(h_ref, wu_ref, wva_ref, wza_ref, wq_ref, wk_ref, wv_ref, wz_ref,
                       cos_ref, sin_ref, wm_ref, bs_ref, lng_ref, retg_ref, *rest, tm, rc, aliased):
    rest = rest[1:] if aliased else rest
    ya_ref, yb_ref, sfin_ref, s_scr, dec_scr, qd_scr, kd_scr = rest
    b = pl.program_id(0)
    j = pl.program_id(1)
    i = pl.program_id(2)

    @pl.when((b == 0) & (j == 0) & (i == 0))
    def _():
        _fill_retention_tables(dec_scr, qd_scr, kd_scr, rc)

    n_chunks = tm // rc

    def projection(c, w_refs):
        rows = pl.ds(pl.multiple_of(i * tm + c * rc, rc), rc)
        p = [None] * len(w_refs)

        def piece(n):
            def run():
                p[n] = _dot(h_ref[rows, :], w_refs[n][0])
            return run

        return p, [piece(n) for n in range(len(w_refs))]

    @pl.when(j < A_HEADS)
    def _():
        w_refs = (wva_ref, wu_ref, wza_ref)
        cur, pieces = projection(0, w_refs)
        while pieces:
            _run_next(pieces)
        for c in range(n_chunks):
            nxt, pieces = projection(c + 1, w_refs) if c + 1 < n_chunks else (None, [])
            y, _ = _gmlp_unit(cur[1], cur[0], cur[2], lng_ref[0], wm_ref[0, 0], bs_ref[0, 0], GMLP_CHUNK, pieces)
            ya_ref[c * rc:(c + 1) * rc, :] = y.astype(ya_ref.dtype)
            cur = nxt

    @pl.when((j >= A_HEADS) & (i == 0))
    def _():
        s_scr[...] = jnp.zeros_like(s_scr)

    @pl.when(j >= A_HEADS)
    def _():
        hd = j - A_HEADS
        s = s_scr[...]
        dec = dec_scr[hd]
        qd = qd_scr[hd]
        kd = kd_scr[hd]
        w_refs = (wq_ref, wk_ref, wv_ref, wz_ref)
        cur, pieces = projection(0, w_refs)
        while pieces:
            _run_next(pieces)
        for c in range(n_chunks):
            rows = slice(c * rc, (c + 1) * rc)
            nxt, pieces = projection(c + 1, w_refs) if c + 1 < n_chunks else (None, [])
            y, s = _retention_chunk(cur[0], cur[1], cur[2], cur[3], s, cos_ref[rows, :], sin_ref[rows, :],
                                    dec, qd, kd, retg_ref[0], pieces)
            yb_ref[rows, :] = y.astype(yb_ref.dtype)
            cur = nxt
        s_scr[...] = s
        sfin_ref[0, 0, 0] = s
        _zero_other_slots(sfin_ref)


def _fused_even(h, w_in, cos, sin, wm, bs, ln_g3, ret_g3, prev_state, *, layer, batch, length):
    li = layer // 2
    tm, rc = PROMPT_TM, PROMPT_RC
    tpb = length // tm
    t = batch * length
    n_items = A_HEADS + B_HEADS

    def a_of(j):
        return jnp.minimum(j, A_HEADS - 1)

    def h_of(j):
        return jnp.maximum(j - A_HEADS, 0)

    def ya_row(b, j, i):
        return b * tpb + jnp.where(j < A_HEADS, i, tpb - 1)

    def yb_row(b, j, i):
        return b * tpb + jnp.where(j >= A_HEADS, i, 0)

    def wa_spec(seg):
        return pl.BlockSpec((1, D_MODEL, A_HEAD_DIM), lambda b, j, i: (li, 0, seg * A_HEADS + a_of(j)))

    def wb_spec(seg):
        base = 3 * A_WIDTH // B_DK + seg * B_HEADS
        return pl.BlockSpec((1, D_MODEL, B_DK), lambda b, j, i: (li, 0, base + h_of(j)))

    rope_mode = pl.Buffered(1) if tpb == 1 else None
    in_specs = [
        pl.BlockSpec((length, D_MODEL), lambda b, j, i: (b, 0)),
        wa_spec(0), wa_spec(1), wa_spec(2),
        wb_spec(0), wb_spec(1), wb_spec(2), wb_spec(3),
        pl.BlockSpec((tm, B_DK // 2), lambda b, j, i: (i, 0), pipeline_mode=rope_mode),
        pl.BlockSpec((tm, B_DK // 2), lambda b, j, i: (i, 0), pipeline_mode=rope_mode),
        pl.BlockSpec((1, 1, GMLP_CHUNK, GMLP_CHUNK), lambda b, j, i: (li, a_of(j), 0, 0)),
        pl.BlockSpec((1, 1, GMLP_CHUNK, A_HEAD_DIM), lambda b, j, i: (li, a_of(j), 0, 0)),
        pl.BlockSpec((1, 1, A_HEAD_DIM), lambda b, j, i: (li, 0, a_of(j))),
        pl.BlockSpec((1, 1, B_DV), lambda b, j, i: (li, 0, 0)),
    ]
    args = [h, w_in, w_in, w_in, w_in, w_in, w_in, w_in, cos, sin, wm, bs, ln_g3, ret_g3]
    aliases = _alias_prev(in_specs, args, prev_state, 2)
    n_state = DEPTH // 2
    return pl.pallas_call(
        functools.partial(_fused_even_kernel, tm=tm, rc=rc, aliased=prev_state is not None),
        grid=(batch, n_items, tpb),
        in_specs=in_specs,
        out_specs=[
            pl.BlockSpec((tm, A_HEAD_DIM), lambda b, j, i: (ya_row(b, j, i), a_of(j))),
            pl.BlockSpec((tm, B_DV), lambda b, j, i: (yb_row(b, j, i), h_of(j))),
            _stacked_out_spec(n_state, li, prev_state is None, (1, 1, B_DK, B_DV),
                              lambda b, j, i: (b, h_of(j), 0, 0)),
        ],
        out_shape=[
            jax.ShapeDtypeStruct((t, A_WIDTH), BF16),
            jax.ShapeDtypeStruct((t, B_WIDTH), BF16),
            jax.ShapeDtypeStruct((n_state, batch, B_HEADS, B_DK, B_DV), F32),
        ],
        scratch_shapes=[
            pltpu.VMEM((B_DK, B_DV), F32),
            pltpu.VMEM((B_HEADS, rc, rc), F32),
            pltpu.VMEM((B_HEADS, rc, B_DK // 2), F32),
            pltpu.VMEM((B_HEADS, rc, B_DK // 2), F32),
        ],
        input_output_aliases=aliases,
        compiler_params=_params("arbitrary", "arbitrary", "arbitrary"),
        name="fused_even",
    )(*args)


def _fused_odd_kernel(h_ref, wq_ref, wk_ref, wv_ref, wz_ref, wlr_ref, wgk_ref, bgk_ref,
                      glag_ref, *rest, tm, rc, blk, aliased):
    rest = rest[1:] if aliased else rest
    y_ref, sfin_ref, lr_scr, st_scr = rest
    hd = pl.program_id(1)
    i = pl.program_id(2)

    @pl.when(hd == 0)
    def _():
        lr_scr[i] = _dot(h_ref[pl.ds(pl.multiple_of(i * tm, tm), tm), :], wlr_ref[0]).astype(BF16)

    @pl.when(i == 0)
    def _():
        st_scr[...] = jnp.zeros_like(st_scr)

    def projection(c):
        rows = slice(c * rc, (c + 1) * rc)
        grows = pl.ds(pl.multiple_of(i * tm + c * rc, rc), rc)
        p = {}

        def piece_q():
            p["q"] = _dot(h_ref[grows, :], wq_ref[0]) * (C_DK ** -0.5)
            p["gk"] = _log_sigmoid(_dot(lr_scr[i, rows, :], wgk_ref[0]) + bgk_ref[0]) * (1.0 / GLA_GATE_NORM)

        def piece_k():
            p["k"] = _dot(h_ref[grows, :], wk_ref[0])

        def piece_v():
            p["v"] = _dot(h_ref[grows, :], wv_ref[0]).astype(BF16)

        def piece_z():
            p["z"] = _dot(h_ref[grows, :], wz_ref[0])

        return p, [piece_v, piece_q, piece_z, piece_k]

    tril = _block_tril(rc, rc)
    st = st_scr[...]
    n_chunks = tm // rc
    cur, pieces = projection(0)
    while pieces:
        _run_next(pieces)
    for c in range(n_chunks):
        nxt, pieces = projection(c + 1) if c + 1 < n_chunks else (None, [])
        y, st = _gla_chunk_anchored(cur["q"], cur["k"], cur["v"], cur["z"], cur["gk"], tril, st, glag_ref[0], blk,
                                    pieces)
        y_ref[c * rc:(c + 1) * rc, :] = y.astype(y_ref.dtype)
        cur = nxt
    st_scr[...] = st
    sfin_ref[0, 0, 0] = st.T
    _zero_other_slots(sfin_ref)


def _fused_odd(h, w_in, w_lr, w_gk, b_gk3, gla_g3, prev_state, *, layer, batch, length):
    li = layer // 2
    tm, rc, blk = PROMPT_TM, PROMPT_RC, PROMPT_GLA_BLK
    tpb = length // tm
    t = batch * length
    v_base = 2 * C_HEADS * C_DK // C_DV

    in_specs = [
        pl.BlockSpec((length, D_MODEL), lambda b, h, i: (b, 0)),
        pl.BlockSpec((1, D_MODEL, C_DK), lambda b, h, i: (li, 0, h)),
        pl.BlockSpec((1, D_MODEL, C_DK), lambda b, h, i: (li, 0, C_HEADS + h)),
        pl.BlockSpec((1, D_MODEL, C_DV), lambda b, h, i: (li, 0, v_base + h)),
        pl.BlockSpec((1, D_MODEL, C_DV), lambda b, h, i: (li, 0, v_base + C_HEADS + h)),
        pl.BlockSpec((1, D_MODEL, LR_PAD), lambda b, h, i: (li, 0, 0)),
        pl.BlockSpec((1, LR_PAD, C_DK), lambda b, h, i: (li, 0, h)),
        pl.BlockSpec((1, 1, C_DK), lambda b, h, i: (li, 0, h)),
        pl.BlockSpec((1, 1, C_DV), lambda b, h, i: (li, 0, 0)),
    ]
    args = [h, w_in, w_in, w_in, w_in, w_lr, w_gk, b_gk3, gla_g3]
    aliases = _alias_prev(in_specs, args, prev_state, 1)
    n_state = DEPTH // 2
    return pl.pallas_call(
        functools.partial(_fused_odd_kernel, tm=tm, rc=rc, blk=blk, aliased=prev_state is not None),
        grid=(batch, C_HEADS, tpb),
        in_specs=in_specs,
        out_specs=[
            pl.BlockSpec((tm, C_DV), lambda b, h, i: (b * tpb + i, h)),
            _stacked_out_spec(n_state, li, prev_state is None, (1, 1, C_DK, C_DV), lambda b, h, i: (b, h, 0, 0)),
        ],
        out_shape=[
            jax.ShapeDtypeStruct((t, C_WIDTH), BF16),
            jax.ShapeDtypeStruct((n_state, batch, C_HEADS, C_DK, C_DV), F32),
        ],
        scratch_shapes=[
            pltpu.VMEM((tpb, tm, LR_PAD), BF16),
            pltpu.VMEM((C_DV, C_DK), F32),
        ],
        input_output_aliases=aliases,
        compiler_params=_params("arbitrary", "arbitrary", "arbitrary"),
        name="fused_odd",
    )(*args)


def _first_norm_kernel(x_ref, sh_ref, sc_ref, g_ref, h_ref):
    h_ref[...] = _norm_modulate(x_ref[...], g_ref[0], sc_ref[...], sh_ref[...])


def _first_norm(x, mod5, norm_g3, *, length):
    t, d = x.shape
    tm = OUT_NORM_TM
    tpb = length // tm

    def mod_spec(which):
        return pl.BlockSpec((None, None, None, 1, d), lambda i: (0, i // tpb, which, 0, 0))

    return pl.pallas_call(
        _first_norm_kernel,
        grid=(t // tm,),
        in_specs=[pl.BlockSpec((tm, d), lambda i: (i, 0)), mod_spec(0), mod_spec(1),
                  pl.BlockSpec((1, 1, d), lambda i: (0, 0, 0))],
        out_specs=pl.BlockSpec((tm, d), lambda i: (i, 0)),
        out_shape=jax.ShapeDtypeStruct((t, d), BF16),
        compiler_params=_params("arbitrary"),
        name="first_norm",
    )(x, mod5, mod5, norm_g3)


def _out_norm_kernel(*refs, n_parts, n_chunks, last):
    y_refs = refs[:n_parts]
    w_refs = refs[n_parts:2 * n_parts]
    if last:
        x_ref, gate_ref, g_ref, o_ref, xn_ref = refs[2 * n_parts:]
    else:
        x_ref, gate_ref, g_ref, sh_ref, sc_ref, xn_ref, h_ref = refs[2 * n_parts:]
    tm, d = x_ref.shape
    tn = d // n_chunks
    rg = OUT_NORM_ROWS
    ssq = [0.0] * (tm // rg)

    def residual(g, n, acc):
        rows, cols = slice(g * rg, (g + 1) * rg), slice(n * tn, (n + 1) * tn)
        xn = x_ref[rows, cols] + gate_ref[:, cols] * acc
        xn_ref[rows, cols] = xn
        ssq[g] = ssq[g] + jnp.sum(xn * xn, axis=-1, keepdims=True)

    def normalise(g):
        rows = slice(g * rg, (g + 1) * rg)
        r = lax.rsqrt(ssq[g] * (1.0 / d) + EPS)
        for n in range(n_chunks):
            cols = slice(n * tn, (n + 1) * tn)
            y = xn_ref[rows, cols] * r * g_ref[0][:, cols]
            if last:
                o_ref[rows, cols] = y
            else:
                h_ref[rows, cols] = (y * (1.0 + sc_ref[:, cols]) + sh_ref[:, cols]).astype(h_ref.dtype)

    pending = []
    for g in range(tm // rg):
        rows = slice(g * rg, (g + 1) * rg)
        for n in range(n_chunks):
            cols = slice(n * tn, (n + 1) * tn)
            acc = _dot(y_refs[0][rows, :], w_refs[0][0, :, cols])
            for p in range(1, n_parts):
                acc = acc + _dot(y_refs[p][rows, :], w_refs[p][0, :, cols])
            while pending:
                _run_next(pending)
            pending.append(functools.partial(residual, g, n, acc))
        pending.append(functools.partial(normalise, g))
    while pending:
        _run_next(pending)


def _out_norm(y_parts, w, x, mod5, norm_g3, final_g, *, layer, length):
    li = layer // 2
    last = layer == DEPTH - 1
    t, d = x.shape
    tm = OUT_NORM_TM
    tpb = length // tm
    n_parts = len(y_parts)
    kp = w.shape[1] // n_parts

    def mod_spec(lyr, which):
        return pl.BlockSpec((None, None, None, 1, d), lambda i: (lyr, i // tpb, which, 0, 0))

    row_spec = pl.BlockSpec((tm, d), lambda i: (i, 0))
    in_specs = [pl.BlockSpec((tm, kp), lambda i: (i, 0)) for _ in range(n_parts)]
    in_specs += [pl.BlockSpec((1, kp, d), functools.partial(lambda i, p: (li, p, 0), p=p),
                              pipeline_mode=pl.Buffered(1)) for p in range(n_parts)]
    in_specs += [row_spec, mod_spec(layer, 2)]
    args = [*y_parts, *([w] * n_parts), x, mod5]
    if last:
        in_specs.append(pl.BlockSpec((1, 1, d), lambda i: (0, 0, 0)))
        args.append(final_g.reshape(1, 1, d))
        out_specs = [row_spec]
        out_shape = [jax.ShapeDtypeStruct((t, d), F32)]
        scratch = [pltpu.VMEM((tm, d), F32)]
    else:
        in_specs += [pl.BlockSpec((1, 1, d), lambda i: (layer + 1, 0, 0)),
                     mod_spec(layer + 1, 0), mod_spec(layer + 1, 1)]
        args += [norm_g3, mod5, mod5]
        out_specs = [row_spec, row_spec]
        out_shape = [jax.ShapeDtypeStruct((t, d), F32), jax.ShapeDtypeStruct((t, d), BF16)]
        scratch = []
    return pl.pallas_call(
        functools.partial(_out_norm_kernel, n_parts=n_parts, n_chunks=d // OUT_TN, last=last),
        grid=(t // tm,),
        in_specs=in_specs,
        out_specs=out_specs,
        out_shape=out_shape,
        scratch_shapes=scratch,
        compiler_params=_params("arbitrary"),
        name="out_norm",
    )(*args)


def _out_proj_kernel(*refs, n_parts):
    y_refs = refs[:n_parts]
    w_refs = refs[n_parts:2 * n_parts]
    x_ref, gate_ref, o_ref = refs[2 * n_parts:]
    acc = _dot(y_refs[0][...], w_refs[0][0])
    for p in range(1, n_parts):
        acc = acc + _dot(y_refs[p][...], w_refs[p][0])
    o_ref[...] = x_ref[...] + gate_ref[...] * acc


def _out_proj(y_parts, w, x, gate_arr, gate_spec, *, layer, tm):
    li = layer // 2
    t, d = x.shape
    tn = OUT_TN
    n_parts = len(y_parts)
    kp = w.shape[1] // n_parts
    in_specs = [pl.BlockSpec((tm, kp), lambda i, j: (i, 0)) for _ in range(n_parts)]
    in_specs += [pl.BlockSpec((1, kp, tn), functools.partial(lambda i, j, p: (li, p, j), p=p)) for p in range(n_parts)]
    in_specs += [pl.BlockSpec((tm, tn), lambda i, j: (i, j)), gate_spec]
    return pl.pallas_call(
        functools.partial(_out_proj_kernel, n_parts=n_parts),
        grid=(t // tm, d // tn),
        in_specs=in_specs,
        out_specs=pl.BlockSpec((tm, tn), lambda i, j: (i, j)),
        out_shape=jax.ShapeDtypeStruct((t, d), F32),
        compiler_params=_params("arbitrary", "arbitrary"),
        name="out_proj",
    )(*y_parts, *([w] * n_parts), x, gate_arr)


def _final_norm_kernel(x_ref, g_ref, o_ref):
    o_ref[...] = _rms(x_ref[...], g_ref[...])


def _final_norm(x, g, *, tm):
    t, d = x.shape
    return pl.pallas_call(
        _final_norm_kernel,
        grid=(t // tm,),
        in_specs=[pl.BlockSpec((tm, d), lambda i: (i, 0)), pl.BlockSpec((1, d), lambda i: (0, 0))],
        out_specs=pl.BlockSpec((tm, d), lambda i: (i, 0)),
        out_shape=jax.ShapeDtypeStruct((t, d), F32),
        compiler_params=_params("arbitrary"),
        name="final_norm",
    )(x, g)


def _in_proj_kernel(x_ref, sh_ref, sc_ref, g_ref, w_ref, *rest, has_lr):
    if has_lr:
        wlr_ref, o_ref, lr_ref, h_scr = rest
    else:
        o_ref, h_scr = rest

    @pl.when(pl.program_id(0) == 0)
    def _():
        h = _norm_modulate(x_ref[...], g_ref[0], sc_ref[...], sh_ref[...])
        h_scr[...] = h
        if has_lr:
            lr_ref[...] = _dot(h, wlr_ref[0]).astype(BF16)

    o_ref[...] = _dot(h_scr[...], w_ref[0]).astype(o_ref.dtype)


def _in_proj(x, mod_rows, norm_g3, w, wlr, *, layer):
    li = layer // 2
    t, d = x.shape
    tn = PROJ_TN
    n = w.shape[2] // tn * tn
    has_lr = wlr is not None
    in_specs = [
        pl.BlockSpec((t, d), lambda j: (0, 0)),
        pl.BlockSpec((None, None, t, d), lambda j: (layer, 0, 0, 0)),
        pl.BlockSpec((None, None, t, d), lambda j: (layer, 1, 0, 0)),
        pl.BlockSpec((1, 1, d), lambda j: (layer, 0, 0)),
        pl.BlockSpec((1, d, tn), lambda j: (li, 0, j)),
    ]
    args = [x, mod_rows, mod_rows, norm_g3, w]
    out_specs = [pl.BlockSpec((t, tn), lambda j: (0, j))]
    out_shape = [jax.ShapeDtypeStruct((t, n), BF16)]
    if has_lr:
        in_specs.append(pl.BlockSpec((1, d, LR_PAD), lambda j: (li, 0, 0)))
        args.append(wlr)
        out_specs.append(pl.BlockSpec((t, LR_PAD), lambda j: (0, 0)))
        out_shape.append(jax.ShapeDtypeStruct((t, LR_PAD), BF16))
    return pl.pallas_call(
        functools.partial(_in_proj_kernel, has_lr=has_lr),
        grid=(n // tn,),
        in_specs=in_specs,
        out_specs=out_specs,
        out_shape=out_shape,
        scratch_shapes=[pltpu.VMEM((t, d), BF16)],
        compiler_params=_params("arbitrary"),
        name="in_proj",
    )(*args)


def _even_mix_kernel(p_ref, cos_ref, sin_ref, wm_ref, bs_ref, lng_ref, retg_ref, s0_ref, *rest, ts, aliased):
    rest = rest[2:] if aliased else rest
    y_ref, sfin_ref, va_ref, dec_scr, qd_scr, kd_scr = rest

    @pl.when(pl.program_id(0) == 0)
    def _():
        _fill_retention_tables(dec_scr, qd_scr, kd_scr, ts)

    _zero_other_slots(sfin_ref)
    _zero_other_slots(va_ref)
    for a in range(A_HEADS):
        c0 = a * A_HEAD_DIM
        u = p_ref[:, c0:c0 + A_HEAD_DIM].astype(F32)
        v = p_ref[:, A_WIDTH + c0:A_WIDTH + c0 + A_HEAD_DIM].astype(F32)
        z = p_ref[:, 2 * A_WIDTH + c0:2 * A_WIDTH + c0 + A_HEAD_DIM].astype(F32)
        y, vn = _gmlp_unit(u, v, z, lng_ref[0, :, c0:c0 + A_HEAD_DIM], wm_ref[0, a], bs_ref[0, a], ts)
        va_ref[0, :, c0:c0 + A_HEAD_DIM] = vn
        y_ref[:, c0:c0 + A_HEAD_DIM] = y.astype(y_ref.dtype)

    q0 = 3 * A_WIDTH
    k0 = q0 + B_HEADS * B_DK
    v0 = k0 + B_HEADS * B_DK
    z0 = v0 + B_WIDTH
    cos = cos_ref[...]
    sin = sin_ref[...]
    for h in range(B_HEADS):
        hc = h * B_DK
        y, s_new = _retention_chunk(
            p_ref[:, q0 + hc:q0 + hc + B_DK].astype(F32), p_ref[:, k0 + hc:k0 + hc + B_DK].astype(F32),
            p_ref[:, v0 + hc:v0 + hc + B_DV].astype(F32), p_ref[:, z0 + hc:z0 + hc + B_DV].astype(F32),
            s0_ref[0, 0, h], cos, sin, dec_scr[h], qd_scr[h], kd_scr[h], retg_ref[0])
        sfin_ref[0, 0, h] = s_new
        y_ref[:, A_WIDTH + hc:A_WIDTH + hc + B_DV] = y.astype(y_ref.dtype)


def _even_mix(p, cos, sin, wm, bs, ln_g3, ret_g3, state, prev_state, prev_va, *, layer, batch, ts):
    li = layer // 2
    n_state = DEPTH // 2
    in_specs = [
        pl.BlockSpec((ts, E_IN), lambda b: (b, 0)),
        pl.BlockSpec((ts, B_DK // 2), lambda b: (0, 0)),
        pl.BlockSpec((ts, B_DK // 2), lambda b: (0, 0)),
        pl.BlockSpec((1, A_HEADS, ts, ts), lambda b: (li, 0, 0, 0)),
        pl.BlockSpec((1, A_HEADS, ts, A_HEAD_DIM), lambda b: (li, 0, 0, 0)),
        pl.BlockSpec((1, 1, A_WIDTH), lambda b: (li, 0, 0)),
        pl.BlockSpec((1, 1, B_DV), lambda b: (li, 0, 0)),
        pl.BlockSpec((1, 1, B_HEADS, B_DK, B_DV), lambda b: (li, b, 0, 0, 0)),
    ]
    args = [p, cos, sin, wm, bs, ln_g3, ret_g3, state]
    aliases = _alias_prev(in_specs, args, prev_state, 1)
    aliases.update(_alias_prev(in_specs, args, prev_va, 2))
    return pl.pallas_call(
        functools.partial(_even_mix_kernel, ts=ts, aliased=prev_state is not None),
        grid=(batch,),
        in_specs=in_specs,
        out_specs=[
            pl.BlockSpec((ts, E_MIX), lambda b: (b, 0)),
            _stacked_out_spec(n_state, li, prev_state is None, (1, B_HEADS, B_DK, B_DV), lambda b: (b, 0, 0, 0)),
            _stacked_out_spec(n_state, li, prev_va is None, (ts, A_WIDTH), lambda b: (b, 0)),
        ],
        out_shape=[
            jax.ShapeDtypeStruct((batch * ts, E_MIX), BF16),
            jax.ShapeDtypeStruct((n_state, batch, B_HEADS, B_DK, B_DV), F32),
            jax.ShapeDtypeStruct((n_state, batch * ts, A_WIDTH), F32),
        ],
        scratch_shapes=[
            pltpu.VMEM((B_HEADS, ts, ts), F32),
            pltpu.VMEM((B_HEADS, ts, B_DK // 2), F32),
            pltpu.VMEM((B_HEADS, ts, B_DK // 2), F32),
        ],
        input_output_aliases=aliases,
        compiler_params=_params("arbitrary"),
        name="even_mix",
    )(*args)


def _odd_mix_kernel(p_ref, lr_ref, wgk_ref, bgk_ref, glag_ref, s0_ref, *rest, ts, aliased):
    rest = rest[1:] if aliased else rest
    y_ref, sfin_ref = rest
    _zero_other_slots(sfin_ref)
    gk = _log_sigmoid(_dot(lr_ref[...], wgk_ref[0]) + bgk_ref[0]) * (1.0 / GLA_GATE_NORM)
    tril = _block_tril(ts, ts)
    causal = _causal(ts)
    k0 = C_HEADS * C_DK
    v0 = 2 * C_HEADS * C_DK
    z0 = v0 + C_WIDTH
    for h in range(C_HEADS):
        kc = h * C_DK
        vc = h * C_DV
        bcum = _cumsum_rows(gk[:, kc:kc + C_DK], tril)
        q = p_ref[:, kc:kc + C_DK].astype(F32) * (C_DK ** -0.5)
        k = p_ref[:, k0 + kc:k0 + kc + C_DK].astype(F32)
        v = p_ref[:, v0 + vc:v0 + vc + C_DV]
        z = p_ref[:, z0 + vc:z0 + vc + C_DV].astype(F32)
        y, st = _gla_chunk(q, k, v, z, bcum, s0_ref[0, 0, h].T, causal, glag_ref[0])
        sfin_ref[0, 0, h] = st.T
        y_ref[:, vc:vc + C_DV] = y.astype(y_ref.dtype)


def _odd_mix(p, lr, w_gk, b_gk3, gla_g3, state, prev_state, *, layer, batch, ts):
    li = layer // 2
    n_state = DEPTH // 2
    in_specs = [
        pl.BlockSpec((ts, O_MAIN), lambda b: (b, 0)),
        pl.BlockSpec((ts, LR_PAD), lambda b: (b, 0)),
        pl.BlockSpec((1, LR_PAD, C_HEADS * C_DK), lambda b: (li, 0, 0)),
        pl.BlockSpec((1, 1, C_HEADS * C_DK), lambda b: (li, 0, 0)),
        pl.BlockSpec((1, 1, C_DV), lambda b: (li, 0, 0)),
        pl.BlockSpec((1, 1, C_HEADS, C_DK, C_DV), lambda b: (li, b, 0, 0, 0)),
    ]
    args = [p, lr, w_gk, b_gk3, gla_g3, state]
    aliases = _alias_prev(in_specs, args, prev_state, 1)
    return pl.pallas_call(
        functools.partial(_odd_mix_kernel, ts=ts, aliased=prev_state is not None),
        grid=(batch,),
        in_specs=in_specs,
        out_specs=[
            pl.BlockSpec((ts, C_WIDTH), lambda b: (b, 0)),
            _stacked_out_spec(n_state, li, prev_state is None, (1, C_HEADS, C_DK, C_DV), lambda b: (b, 0, 0, 0)),
        ],
        out_shape=[
            jax.ShapeDtypeStruct((batch * ts, C_WIDTH), BF16),
            jax.ShapeDtypeStruct((n_state, batch, C_HEADS, C_DK, C_DV), F32),
        ],
        input_output_aliases=aliases,
        compiler_params=_params("arbitrary"),
        name="odd_mix",
    )(*args)


def _rope_tables(pos):
    half = B_DK // 2
    freqs = ROPE_BASE ** (-jnp.arange(half, dtype=F32) / half)
    ang = pos.astype(F32)[:, None] * freqs[None, :]
    return jnp.cos(ang), jnp.sin(ang)


def _gmlp_weights(gmlp_ws, gmlp_bs, seg):
    p = np.arange(seg)
    mask = (p[None, :] // CHUNK) <= (p[:, None] // CHUNK)
    wm = jnp.where(mask[None, None], gmlp_ws[:, :, :seg, :seg], 0.0).astype(BF16)
    bs = jnp.broadcast_to(gmlp_bs[:, :, :seg, None], gmlp_bs.shape[:2] + (seg, A_HEAD_DIM))
    return wm, bs


def _prompt_trunk(x, mod5, w, *, batch, length):
    cos, sin = _rope_tables(jnp.arange(length))
    wm, bs = _gmlp_weights(w["gmlp_ws"], w["gmlp_bs"], GMLP_CHUNK)
    ret, gla = None, None
    h = _first_norm(x, mod5, w["norm_g"], length=length)
    for layer in range(DEPTH):
        if layer % 2 == 0:
            ya, yb, ret = _fused_even(h, w["w_in_e"], cos, sin, wm, bs, w["gmlp_ln_g"], w["ret_norm_g"], ret,
                                      layer=layer, batch=batch, length=length)
            outs = _out_norm([ya, yb], w["w_out_e"], x, mod5, w["norm_g"], w["final_g"], layer=layer, length=length)
        else:
            y, gla = _fused_odd(h, w["w_in_o"], w["w_lr"], w["w_gk"], w["b_gk"], w["gla_norm_g"], gla,
                                layer=layer, batch=batch, length=length)
            outs = _out_norm([y], w["w_out_o"], x, mod5, w["norm_g"], w["final_g"], layer=layer, length=length)
        if layer < DEPTH - 1:
            x, h = outs
    return outs[0], ret, gla


def _sample_trunk(x, mod_rows, state_ret, state_gla, w, *, batch, length):
    t = batch * length
    cos, sin = _rope_tables(PAST_LEN + jnp.arange(length))
    wm, bs = _gmlp_weights(w["gmlp_ws"], w["gmlp_bs"], length)
    gate_spec = lambda layer: pl.BlockSpec((None, None, t, OUT_TN), lambda i, j: (layer, 2, 0, j))
    ret, gla, va = None, None, None
    for layer in range(DEPTH):
        if layer % 2 == 0:
            (p,) = _in_proj(x, mod_rows, w["norm_g"], w["w_in_e"], None, layer=layer)
            y, ret, va = _even_mix(p, cos, sin, wm, bs, w["gmlp_ln_g"], w["ret_norm_g"], state_ret, ret, va,
                                   layer=layer, batch=batch, ts=length)
            x = _out_proj([y], w["w_out_e"], x, mod_rows, gate_spec(layer), layer=layer, tm=t)
        else:
            p, lr = _in_proj(x, mod_rows, w["norm_g"], w["w_in_o"], w["w_lr"], layer=layer)
            y, gla = _odd_mix(p, lr, w["w_gk"], w["b_gk"], w["gla_norm_g"], state_gla, gla,
                              layer=layer, batch=batch, ts=length)
            x = _out_proj([y], w["w_out_o"], x, mod_rows, gate_spec(layer), layer=layer, tm=t)
    return _final_norm(x, w["final_g"], tm=t), ret, gla, va


def kernel(x_prompt, x_sample, state_ret, state_gla, c_prompt, c_sample, norm_g, w_mod, b_mod, w_in_e, gmlp_ln_g,
           gmlp_ws, gmlp_bs, ret_norm_g, w_out_e, w_in_o, w_gk2, b_gk, gla_norm_g, w_out_o, final_g):
    bp, lp, d = x_prompt.shape
    bs_, ls = x_sample.shape[:2]
    n_even, n_odd = w_in_e.shape[0], w_in_o.shape[0]

    w = {
        "norm_g": norm_g.reshape(DEPTH, 1, d),
        "w_in_e": w_in_e.astype(BF16),
        "w_out_e": w_out_e.astype(BF16),
        "w_in_o": _cast_leading_columns(w_in_o, O_MAIN, rows_per_block=128),
        "w_lr": jnp.pad(w_in_o[:, :, O_MAIN:], ((0, 0), (0, 0), (0, LR_PAD - GLA_RANK))).astype(BF16),
        "w_gk": jnp.pad(w_gk2, ((0, 0), (0, LR_PAD - GLA_RANK), (0, 0))).astype(BF16),
        "w_out_o": w_out_o.astype(BF16),
        "gmlp_ws": gmlp_ws, "gmlp_bs": gmlp_bs,
        "gmlp_ln_g": gmlp_ln_g.reshape(n_even, 1, A_WIDTH),
        "ret_norm_g": ret_norm_g.reshape(n_even, 1, B_DV),
        "b_gk": b_gk.reshape(n_odd, 1, C_HEADS * C_DK),
        "gla_norm_g": gla_norm_g.reshape(n_odd, 1, C_DV),
        "final_g": final_g.reshape(1, d),
    }

    c_all = jnp.concatenate([c_prompt, c_sample], axis=0)
    mod = _modulation(c_all, w_mod, b_mod)
    mod5 = mod.reshape(DEPTH, bp + bs_, 3, 1, d)
    mod_rows = jnp.repeat(mod[:, bp:].reshape(DEPTH, bs_, 3, d).transpose(0, 2, 1, 3), ls, axis=2)

    y_p, ret_p, gla_p = _prompt_trunk(x_prompt.reshape(bp * lp, d), mod5, w, batch=bp, length=lp)
    y_s, ret_s, gla_s, va = _sample_trunk(x_sample.reshape(bs_ * ls, d), mod_rows, state_ret, state_gla, w,
                                          batch=bs_, length=ls)
    return (y_p.reshape(bp, lp, d), y_s.reshape(bs_, ls, d), ret_p, ret_s, gla_p, gla_s,
            va.reshape(n_even, bs_, ls, A_WIDTH))
```

```python
import functools
import math

import jax
import jax.numpy as jnp
import numpy as np
from jax import lax
from jax.experimental import pallas as pl
from jax.experimental.pallas import tpu as pltpu

F32 = jnp.float32
BF16 = jnp.bfloat16

D_MODEL = 2048
DEPTH = 4
PAST_LEN = 1024
EPS = 1e-6
CHUNK = 64
A_HEADS = 4
A_HEAD_DIM = 512
A_WIDTH = 2048
GMLP_CHUNK = 128
B_HEADS = 8
B_DK = 256
B_DV = 256
B_WIDTH = 2048
ROPE_BASE = 10000.0
C_HEADS = 8
C_DK = 256
C_DV = 512
C_WIDTH = 4096
GLA_RANK = 16
GLA_GATE_NORM = 16.0
E_IN = 3 * A_WIDTH + 2 * B_HEADS * B_DK + 2 * B_WIDTH
E_MIX = A_WIDTH + B_WIDTH
O_MAIN = 2 * C_HEADS * C_DK + 2 * C_WIDTH
LR_PAD = 128

VMEM_LIMIT_BYTES = 56 * 1024 * 1024

PROMPT_TM = 2048
PROMPT_RC = 256
PROMPT_GLA_BLK = 16
PROJ_TN = 1024
OUT_TN = 512
OUT_NORM_TM = 512
OUT_NORM_ROWS = 256


def _params(*sem):
    return pltpu.CompilerParams(dimension_semantics=sem, vmem_limit_bytes=VMEM_LIMIT_BYTES)


def _dot(a, b):
    return jnp.dot(a, b, preferred_element_type=F32)


def _dot_nt(a, b):
    return lax.dot_general(a, b, (((1,), (1,)), ((), ())), preferred_element_type=F32)


def _dot_tn(a, b):
    return lax.dot_general(a, b, (((0,), (0,)), ((), ())), preferred_element_type=F32)


def _silu(x):
    return x * jax.nn.sigmoid(x)


def _log_sigmoid(x):
    return jnp.minimum(x, 0.0) - jnp.log1p(jnp.exp(-jnp.abs(x)))


def _rms(x, g):
    return x * lax.rsqrt(jnp.mean(x * x, axis=-1, keepdims=True) + EPS) * g


def _norm_modulate(x, g, scale, shift):
    return (_rms(x, g) * (1.0 + scale) + shift).astype(BF16)


def _retention_log_gamma():
    return [math.log1p(-(2.0 ** (-5.0 - h))) for h in range(B_HEADS)]


def _fill_retention_tables(dec_scr, qd_scr, kd_scr, n):
    log_g = _retention_log_gamma()
    row = lax.broadcasted_iota(jnp.int32, (n, n), 0).astype(F32)
    col = lax.broadcasted_iota(jnp.int32, (n, n), 1).astype(F32)
    diff = row - col
    rowd = lax.broadcasted_iota(jnp.int32, (n, B_DK // 2), 0).astype(F32)
    for h in range(B_HEADS):
        dec_scr[h] = jnp.where(diff >= 0, jnp.exp(jnp.maximum(diff, 0.0) * log_g[h]), 0.0)
        qd_scr[h] = jnp.exp((rowd + 1.0) * log_g[h])
        kd_scr[h] = jnp.exp((n - 1.0 - rowd) * log_g[h])


def _rotary(xh, cos, sin):
    half = xh.shape[-1] // 2
    x1 = xh[:, :half]
    x2 = xh[:, half:]
    return jnp.concatenate([x1 * cos - x2 * sin, x1 * sin + x2 * cos], axis=-1)


def _run_next(fillers):
    if fillers:
        fillers.pop(0)()


def _gmlp_unit(u, v, z, g_ln, wm, bias, seg, fillers=()):
    fillers = list(fillers)
    v = jax.nn.gelu(v)
    vc = v - jnp.mean(v, axis=-1, keepdims=True)
    vn = vc * lax.rsqrt(jnp.mean(vc * vc, axis=-1, keepdims=True) + EPS) * g_ln
    vb = vn.astype(BF16)
    _run_next(fillers)
    mixed = [_dot(wm, vb[s * seg:(s + 1) * seg]) + bias for s in range(u.shape[0] // seg)]
    mixed = mixed[0] if len(mixed) == 1 else jnp.concatenate(mixed, axis=0)
    _run_next(fillers)
    y = jax.nn.gelu(u) * mixed * _silu(z)
    while fillers:
        _run_next(fillers)
    return y, vn


def _retention_chunk(q, k, v, z, s_old, cos, sin, dec, qd, kd, g_ret, fillers=()):
    fillers = list(fillers)
    n = q.shape[0]
    qd = jnp.concatenate([qd, qd], axis=-1)
    kd = jnp.concatenate([kd, kd], axis=-1)
    q = _rotary(q, cos, sin)
    k = _rotary(k, cos, sin) * (B_DK ** -0.5)
    vb = v.astype(BF16)
    _run_next(fillers)
    scores = _dot_nt(q.astype(BF16), k.astype(BF16)) * dec
    s_new = qd[n - 1:n, :] * s_old + _dot_tn((k * kd).astype(BF16), vb)
    _run_next(fillers)
    o = _dot(scores.astype(BF16), vb) + _dot((q * qd).astype(BF16), s_old.astype(BF16))
    _run_next(fillers)
    y = _rms(o, g_ret) * _silu(z)
    while fillers:
        _run_next(fillers)
    return y, s_new


def _gla_chunk(q, k, v, z, bcum, st_old, causal, g_gla):
    n = q.shape[0]
    bend = bcum[n - 1:n, :]
    qf = (q * jnp.exp(bcum)).astype(BF16)
    kf = (k * jnp.exp(-bcum)).astype(BF16)
    scores = jnp.where(causal, _dot_nt(qf, kf), 0.0).astype(BF16)
    o = _dot(scores, v) + _dot_nt(qf, st_old.astype(BF16))
    k_tail = (k * jnp.exp(bend - bcum)).astype(BF16)
    st_new = st_old * jnp.exp(bend) + _dot_tn(v, k_tail)
    return _rms(o, g_gla) * _silu(z), st_new


def _gla_chunk_anchored(q, k, v, z, gk, tril, st_old, g_gla, fine, fillers):
    fillers = list(fillers)
    n = q.shape[0]
    win = 128
    bcum = _cumsum_rows(gk, tril)
    _run_next(fillers)
    bend = bcum[n - 1:n, :]
    score_rows = []
    for w0 in range(0, n, win):
        w1 = w0 + win
        kw, bw = k[w0:w1], bcum[w0:w1]
        row = lax.broadcasted_iota(jnp.int32, (win, C_DK), 0) + w0
        groups = []
        for lo in range(w0, w1, fine):
            hi = lo + fine
            anchor = bcum[lo - 1:lo, :] if lo else jnp.zeros_like(bend)
            qg = (q[lo:hi] * jnp.exp(bcum[lo:hi] - anchor)).astype(BF16)
            kg = (kw * jnp.exp(jnp.where(row < hi, anchor - bw, 0.0))).astype(BF16)
            t_idx = lax.broadcasted_iota(jnp.int32, (fine, win), 0) + lo
            s_idx = lax.broadcasted_iota(jnp.int32, (fine, win), 1) + w0
            groups.append(jnp.where(t_idx >= s_idx, _dot_nt(qg, kg), 0.0).astype(BF16))
        entries = [(jnp.concatenate(groups, axis=0), slice(w0, w1))]
        if w0:
            anchor = bcum[w0 - 1:w0, :]
            qw = (q[w0:w1] * jnp.exp(bw - anchor)).astype(BF16)
            kp = (k[:w0] * jnp.exp(anchor - bcum[:w0])).astype(BF16)
            entries.append((_dot_nt(qw, kp).astype(BF16), slice(0, w0)))
        score_rows.append(entries)
    q_state = (q * jnp.exp(bcum)).astype(BF16)
    k_tail = (k * jnp.exp(bend - bcum)).astype(BF16)
    _run_next(fillers)
    o_rows = []
    for entries in score_rows:
        acc = None
        for s, key_rows in entries:
            part = _dot(s, v[key_rows])
            acc = part if acc is None else acc + part
        o_rows.append(acc)
    o = jnp.concatenate(o_rows, axis=0) + _dot_nt(q_state, st_old.astype(BF16))
    st_new = st_old * jnp.exp(bend) + _dot_tn(v, k_tail)
    _run_next(fillers)
    y = _rms(o, g_gla) * _silu(z)
    while fillers:
        _run_next(fillers)
    return y, st_new


def _cumsum_rows(g, tril):
    g_hi = g.astype(BF16)
    g_lo = (g - g_hi.astype(F32)).astype(BF16)
    return _dot(tril, g_hi) + _dot(tril, g_lo)


def _block_tril(n, blk):
    row = lax.broadcasted_iota(jnp.int32, (n, n), 0)
    col = lax.broadcasted_iota(jnp.int32, (n, n), 1)
    return jnp.where((row >= col) & (row // blk == col // blk), 1.0, 0.0).astype(BF16)


def _causal(n):
    return lax.broadcasted_iota(jnp.int32, (n, n), 0) >= lax.broadcasted_iota(jnp.int32, (n, n), 1)


def _alias_prev(in_specs, args, prev, out_index):
    if prev is None:
        return {}
    in_specs.append(pl.BlockSpec(memory_space=pl.ANY))
    args.append(prev)
    return {len(args) - 1: out_index}


def _stacked_out_spec(n_stack, slot, first, tail_shape, tail_index):
    if first:
        return pl.BlockSpec((n_stack,) + tail_shape, lambda *g: (0,) + tail_index(*g))
    return pl.BlockSpec((1,) + tail_shape, lambda *g: (slot,) + tail_index(*g))


def _zero_other_slots(ref):
    if ref.shape[0] > 1:
        ref[1:] = jnp.zeros((ref.shape[0] - 1,) + ref.shape[1:], ref.dtype)


def _mod_kernel(c_ref, w_ref, b_ref, o_ref):
    s = _silu(c_ref[...]).astype(BF16)
    o_ref[0] = _dot(s, w_ref[0].astype(BF16)) + b_ref[0]


def _modulation(c_all, w_mod, b_mod):
    rows = c_all.shape[0]
    tn = 1024
    n = 3 * D_MODEL
    return pl.pallas_call(
        _mod_kernel,
        grid=(DEPTH, n // tn),
        in_specs=[
            pl.BlockSpec((rows, D_MODEL), lambda l, j: (0, 0)),
            pl.BlockSpec((1, D_MODEL, tn), lambda l, j: (l, 0, j)),
            pl.BlockSpec((1, 1, tn), lambda l, j: (l, 0, j)),
        ],
        out_specs=pl.BlockSpec((1, rows, tn), lambda l, j: (l, 0, j)),
        out_shape=jax.ShapeDtypeStruct((DEPTH, rows, n), F32),
        compiler_params=_params("arbitrary", "arbitrary"),
        name="modulation",
    )(c_all, w_mod, b_mod.reshape(DEPTH, 1, n))


def _fused_even_kernel(h_ref, wu_ref, wva_ref, wza_ref, wq_ref, wk_ref, wv_ref, wz_ref,
                       cos_ref, sin_ref, wm_ref, bs_ref, lng_ref, retg_ref, *rest, tm, rc, aliased):
    rest = rest[1:] if aliased else rest
    ya_ref, yb_ref, sfin_ref, s_scr, dec_scr, qd_scr, kd_scr = rest
    b = pl.program_id(0)
    j = pl.program_id(1)
    i = pl.program_id(2)

    @pl.when((b == 0) & (j == 0) & (i == 0))
    def _():
        _fill_retention_tables(dec_scr, qd_scr, kd_scr, rc)

    n_chunks = tm // rc

    def projection(c, w_refs):
        rows = pl.ds(pl.multiple_of(i * tm + c * rc, rc), rc)
        p = [None] * len(w_refs)

        def piece(n):
            def run():
                p[n] = _dot(h_ref[rows, :], w_refs[n][0])
            return run

        return p, [piece(n) for n in range(len(w_refs))]

    @pl.when(j < A_HEADS)
    def _():
        w_refs = (wva_ref, wu_ref, wza_ref)
        cur, pieces = projection(0, w_refs)
        while pieces:
            _run_next(pieces)
        for c in range(n_chunks):
            nxt, pieces = projection(c + 1, w_refs) if c + 1 < n_chunks else (None, [])
            y, _ = _gmlp_unit(cur[1], cur[0], cur[2], lng_ref[0], wm_ref[0, 0], bs_ref[0, 0], GMLP_CHUNK, pieces)
            ya_ref[c * rc:(c + 1) * rc, :] = y.astype(ya_ref.dtype)
            cur = nxt

    @pl.when((j >= A_HEADS) & (i == 0))
    def _():
        s_scr[...] = jnp.zeros_like(s_scr)

    @pl.when(j >= A_HEADS)
    def _():
        hd = j - A_HEADS
        s = s_scr[...]
        dec = dec_scr[hd]
        qd = qd_scr[hd]
        kd = kd_scr[hd]
        w_refs = (wq_ref, wk_ref, wv_ref, wz_ref)
        cur, pieces = projection(0, w_refs)
        while pieces:
            _run_next(pieces)
        for c in range(n_chunks):
            rows = slice(c * rc, (c + 1) * rc)
            nxt, pieces = projection(c + 1, w_refs) if c + 1 < n_chunks else (None, [])
            y, s = _retention_chunk(cur[0], cur[1], cur[2], cur[3], s, cos_ref[rows, :], sin_ref[rows, :],
                                    dec, qd, kd, retg_ref[0], pieces)
            yb_ref[rows, :] = y.astype(yb_ref.dtype)
            cur = nxt
        s_scr[...] = s
        sfin_ref[0, 0, 0] = s
        _zero_other_slots(sfin_ref)


def _fused_even(h, w_in, cos, sin, wm, bs, ln_g3, ret_g3, prev_state, *, layer, batch, length):
    li = layer // 2
    tm, rc = PROMPT_TM, PROMPT_RC
    tpb = length // tm
    t = batch * length
    n_items = A_HEADS + B_HEADS

    def a_of(j):
        return jnp.minimum(j, A_HEADS - 1)

    def h_of(j):
        return jnp.maximum(j - A_HEADS, 0)

    def ya_row(b, j, i):
        return b * tpb + jnp.where(j < A_HEADS, i, tpb - 1)

    def yb_row(b, j, i):
        return b * tpb + jnp.where(j >= A_HEADS, i, 0)

    def wa_spec(seg):
        return pl.BlockSpec((1, D_MODEL, A_HEAD_DIM), lambda b, j, i: (li, 0, seg * A_HEADS + a_of(j)))

    def wb_spec(seg):
        base = 3 * A_WIDTH // B_DK + seg * B_HEADS
        return pl.BlockSpec((1, D_MODEL, B_DK), lambda b, j, i: (li, 0, base + h_of(j)))

    rope_mode = pl.Buffered(1) if tpb == 1 else None
    in_specs = [
        pl.BlockSpec((length, D_MODEL), lambda b, j, i: (b, 0)),
        wa_spec(0), wa_spec(1), wa_spec(2),
        wb_spec(0), wb_spec(1), wb_spec(2), wb_spec(3),
        pl.BlockSpec((tm, B_DK // 2), lambda b, j, i: (i, 0), pipeline_mode=rope_mode),
        pl.BlockSpec((tm, B_DK // 2), lambda b, j, i: (i, 0), pipeline_mode=rope_mode),
        pl.BlockSpec((1, 1, GMLP_CHUNK, GMLP_CHUNK), lambda b, j, i: (li, a_of(j), 0, 0)),
        pl.BlockSpec((1, 1, GMLP_CHUNK, A_HEAD_DIM), lambda b, j, i: (li, a_of(j), 0, 0)),
        pl.BlockSpec((1, 1, A_HEAD_DIM), lambda b, j, i: (li, 0, a_of(j))),
        pl.BlockSpec((1, 1, B_DV), lambda b, j, i: (li, 0, 0)),
    ]
    args = [h, w_in, w_in, w_in, w_in, w_in, w_in, w_in, cos, sin, wm, bs, ln_g3, ret_g3]
    aliases = _alias_prev(in_specs, args, prev_state, 2)
    n_state = DEPTH // 2
    return pl.pallas_call(
        functools.partial(_fused_even_kernel, tm=tm, rc=rc, aliased=prev_state is not None),
        grid=(batch, n_items, tpb),
        in_specs=in_specs,
        out_specs=[
            pl.BlockSpec((tm, A_HEAD_DIM), lambda b, j, i: (ya_row(b, j, i), a_of(j))),
            pl.BlockSpec((tm, B_DV), lambda b, j, i: (yb_row(b, j, i), h_of(j))),
            _stacked_out_spec(n_state, li, prev_state is None, (1, 1, B_DK, B_DV),
                              lambda b, j, i: (b, h_of(j), 0, 0)),
        ],
        out_shape=[
            jax.ShapeDtypeStruct((t, A_WIDTH), BF16),
            jax.ShapeDtypeStruct((t, B_WIDTH), BF16),
            jax.ShapeDtypeStruct((n_state, batch, B_HEADS, B_DK, B_DV), F32),
        ],
        scratch_shapes=[
            pltpu.VMEM((B_DK, B_DV), F32),
            pltpu.VMEM((B_HEADS, rc, rc), F32),
            pltpu.VMEM((B_HEADS, rc, B_DK // 2), F32),
            pltpu.VMEM((B_HEADS, rc, B_DK // 2), F32),
        ],
        input_output_aliases=aliases,
        compiler_params=_params("arbitrary", "arbitrary", "arbitrary"),
        name="fused_even",
    )(*args)


def _fused_odd_kernel(h_ref, wq_ref, wk_ref, wv_ref, wz_ref, wlr_ref, wgk_ref, bgk_ref,
                      glag_ref, *rest, tm, rc, blk, aliased):
    rest = rest[1:] if aliased else rest
    y_ref, sfin_ref, lr_scr, st_scr = rest
    hd = pl.program_id(1)
    i = pl.program_id(2)

    @pl.when(hd == 0)
    def _():
        lr_scr[i] = _dot(h_ref[pl.ds(pl.multiple_of(i * tm, tm), tm), :], wlr_ref[0]).astype(BF16)

    @pl.when(i == 0)
    def _():
        st_scr[...] = jnp.zeros_like(st_scr)

    def projection(c):
        rows = slice(c * rc, (c + 1) * rc)
        grows = pl.ds(pl.multiple_of(i * tm + c * rc, rc), rc)
        p = {}

        def piece_q():
            p["q"] = _dot(h_ref[grows, :], wq_ref[0]) * (C_DK ** -0.5)
            p["gk"] = _log_sigmoid(_dot(lr_scr[i, rows, :], wgk_ref[0]) + bgk_ref[0]) * (1.0 / GLA_GATE_NORM)

        def piece_k():
            p["k"] = _dot(h_ref[grows, :], wk_ref[0])

        def piece_v():
            p["v"] = _dot(h_ref[grows, :], wv_ref[0]).astype(BF16)

        def piece_z():
            p["z"] = _dot(h_ref[grows, :], wz_ref[0])

        return p, [piece_v, piece_q, piece_z, piece_k]

    tril = _block_tril(rc, rc)
    st = st_scr[...]
    n_chunks = tm // rc
    cur, pieces = projection(0)
    while pieces:
        _run_next(pieces)
    for c in range(n_chunks):
        nxt, pieces = projection(c + 1) if c + 1 < n_chunks else (None, [])
        y, st = _gla_chunk_anchored(cur["q"], cur["k"], cur["v"], cur["z"], cur["gk"], tril, st, glag_ref[0], blk,
                                    pieces)
        y_ref[c * rc:(c + 1) * rc, :] = y.astype(y_ref.dtype)
        cur = nxt
    st_scr[...] = st
    sfin_ref[0, 0, 0] = st.T
    _zero_other_slots(sfin_ref)


def _fused_odd(h, w_in, w_lr, w_gk, b_gk3, gla_g3, prev_state, *, layer, batch, length):
    li = layer // 2
    tm, rc, blk = PROMPT_TM, PROMPT_RC, PROMPT_GLA_BLK
    tpb = length // tm
    t = batch * length
    v_base = 2 * C_HEADS * C_DK // C_DV

    in_specs = [
        pl.BlockSpec((length, D_MODEL), lambda b, h, i: (b, 0)),
        pl.BlockSpec((1, D_MODEL, C_DK), lambda b, h, i: (li, 0, h)),
        pl.BlockSpec((1, D_MODEL, C_DK), lambda b, h, i: (li, 0, C_HEADS + h)),
        pl.BlockSpec((1, D_MODEL, C_DV), lambda b, h, i: (li, 0, v_base + h)),
        pl.BlockSpec((1, D_MODEL, C_DV), lambda b, h, i: (li, 0, v_base + C_HEADS + h)),
        pl.BlockSpec((1, D_MODEL, LR_PAD), lambda b, h, i: (li, 0, 0)),
        pl.BlockSpec((1, LR_PAD, C_DK), lambda b, h, i: (li, 0, h)),
        pl.BlockSpec((1, 1, C_DK), lambda b, h, i: (li, 0, h)),
        pl.BlockSpec((1, 1, C_DV), lambda b, h, i: (li, 0, 0)),
    ]
    args = [h, w_in, w_in, w_in, w_in, w_lr, w_gk, b_gk3, gla_g3]
    aliases = _alias_prev(in_specs, args, prev_state, 1)
    n_state = DEPTH // 2
    return pl.pallas_call(
        functools.partial(_fused_odd_kernel, tm=tm, rc=rc, blk=blk, aliased=prev_state is not None),
        grid=(batch, C_HEADS, tpb),
        in_specs=in_specs,
        out_specs=[
            pl.BlockSpec((tm, C_DV), lambda b, h, i: (b * tpb + i, h)),
            _stacked_out_spec(n_state, li, prev_state is None, (1, 1, C_DK, C_DV), lambda b, h, i: (b, h, 0, 0)),
        ],
        out_shape=[
            jax.ShapeDtypeStruct((t, C_WIDTH), BF16),
            jax.ShapeDtypeStruct((n_state, batch, C_HEADS, C_DK, C_DV), F32),
        ],
        scratch_shapes=[
            pltpu.VMEM((tpb, tm, LR_PAD), BF16),
            pltpu.VMEM((C_DV, C_DK), F32),
        ],
        input_output_aliases=aliases,
        compiler_params=_params("arbitrary", "arbitrary", "arbitrary"),
        name="fused_odd",
    )(*args)


def _first_norm_kernel(x_ref, sh_ref, sc_ref, g_ref, h_ref):
    h_ref[...] = _norm_modulate(x_ref[...], g_ref[0], sc_ref[...], sh_ref[...])


def _first_norm(x, mod5, norm_g3, *, length):
    t, d = x.shape
    tm = OUT_NORM_TM
    tpb = length // tm

    def mod_spec(which):
        return pl.BlockSpec((None, None, None, 1, d), lambda i: (0, i // tpb, which, 0, 0))

    return pl.pallas_call(
        _first_norm_kernel,
        grid=(t // tm,),
        in_specs=[pl.BlockSpec((tm, d), lambda i: (i, 0)), mod_spec(0), mod_spec(1),
                  pl.BlockSpec((1, 1, d), lambda i: (0, 0, 0))],
        out_specs=pl.BlockSpec((tm, d), lambda i: (i, 0)),
        out_shape=jax.ShapeDtypeStruct((t, d), BF16),
        compiler_params=_params("arbitrary"),
        name="first_norm",
    )(x, mod5, mod5, norm_g3)


def _out_norm_kernel(*refs, n_parts, n_chunks, last):
    y_refs = refs[:n_parts]
    w_refs = refs[n_parts:2 * n_parts]
    if last:
        x_ref, gate_ref, g_ref, o_ref, xn_ref = refs[2 * n_parts:]
    else:
        x_ref, gate_ref, g_ref, sh_ref, sc_ref, xn_ref, h_ref = refs[2 * n_parts:]
    tm, d = x_ref.shape
    tn = d // n_chunks
    rg = OUT_NORM_ROWS
    ssq = [0.0] * (tm // rg)

    def residual(g, n, acc):
        rows, cols = slice(g * rg, (g + 1) * rg), slice(n * tn, (n + 1) * tn)
        xn = x_ref[rows, cols] + gate_ref[:, cols] * acc
        xn_ref[rows, cols] = xn
        ssq[g] = ssq[g] + jnp.sum(xn * xn, axis=-1, keepdims=True)

    def normalise(g):
        rows = slice(g * rg, (g + 1) * rg)
        r = lax.rsqrt(ssq[g] * (1.0 / d) + EPS)
        for n in range(n_chunks):
            cols = slice(n * tn, (n + 1) * tn)
            y = xn_ref[rows, cols] * r * g_ref[0][:, cols]
            if last:
                o_ref[rows, cols] = y
            else:
                h_ref[rows, cols] = (y * (1.0 + sc_ref[:, cols]) + sh_ref[:, cols]).astype(h_ref.dtype)

    pending = []
    for g in range(tm // rg):
        rows = slice(g * rg, (g + 1) * rg)
        for n in range(n_chunks):
            cols = slice(n * tn, (n + 1) * tn)
            acc = _dot(y_refs[0][rows, :], w_refs[0][0, :, cols])
            for p in range(1, n_parts):
                acc = acc + _dot(y_refs[p][rows, :], w_refs[p][0, :, cols])
            while pending:
                _run_next(pending)
            pending.append(functools.partial(residual, g, n, acc))
        pending.append(functools.partial(normalise, g))
    while pending:
        _run_next(pending)


def _out_norm(y_parts, w, x, mod5, norm_g3, final_g, *, layer, length):
    li = layer // 2
    last = layer == DEPTH - 1
    t, d = x.shape
    tm = OUT_NORM_TM
    tpb = length // tm
    n_parts = len(y_parts)
    kp = w.shape[1] // n_parts

    def mod_spec(lyr, which):
        return pl.BlockSpec((None, None, None, 1, d), lambda i: (lyr, i // tpb, which, 0, 0))

    row_spec = pl.BlockSpec((tm, d), lambda i: (i, 0))
    in_specs = [pl.BlockSpec((tm, kp), lambda i: (i, 0)) for _ in range(n_parts)]
    in_specs += [pl.BlockSpec((1, kp, d), functools.partial(lambda i, p: (li, p, 0), p=p),
                              pipeline_mode=pl.Buffered(1)) for p in range(n_parts)]
    in_specs += [row_spec, mod_spec(layer, 2)]
    args = [*y_parts, *([w] * n_parts), x, mod5]
    if last:
        in_specs.append(pl.BlockSpec((1, 1, d), lambda i: (0, 0, 0)))
        args.append(final_g.reshape(1, 1, d))
        out_specs = [row_spec]
        out_shape = [jax.ShapeDtypeStruct((t, d), F32)]
        scratch = [pltpu.VMEM((tm, d), F32)]
    else:
        in_specs += [pl.BlockSpec((1, 1, d), lambda i: (layer + 1, 0, 0)),
                     mod_spec(layer + 1, 0), mod_spec(layer + 1, 1)]
        args += [norm_g3, mod5, mod5]
        out_specs = [row_spec, row_spec]
        out_shape = [jax.ShapeDtypeStruct((t, d), F32), jax.ShapeDtypeStruct((t, d), BF16)]
        scratch = []
    return pl.pallas_call(
        functools.partial(_out_norm_kernel, n_parts=n_parts, n_chunks=d // OUT_TN, last=last),
        grid=(t // tm,),
        in_specs=in_specs,
        out_specs=out_specs,
        out_shape=out_shape,
        scratch_shapes=scratch,
        compiler_params=_params("arbitrary"),
        name="out_norm",
    )(*args)


def _out_proj_kernel(*refs, n_parts):
    y_refs = refs[:n_parts]
    w_refs = refs[n_parts:2 * n_parts]
    x_ref, gate_ref, o_ref = refs[2 * n_parts:]
    acc = _dot(y_refs[0][...], w_refs[0][0])
    for p in range(1, n_parts):
        acc = acc + _dot(y_refs[p][...], w_refs[p][0])
    o_ref[...] = x_ref[...] + gate_ref[...] * acc


def _out_proj(y_parts, w, x, gate_arr, gate_spec, *, layer, tm):
    li = layer // 2
    t, d = x.shape
    tn = OUT_TN
    n_parts = len(y_parts)
    kp = w.shape[1] // n_parts
    in_specs = [pl.BlockSpec((tm, kp), lambda i, j: (i, 0)) for _ in range(n_parts)]
    in_specs += [pl.BlockSpec((1, kp, tn), functools.partial(lambda i, j, p: (li, p, j), p=p)) for p in range(n_parts)]
    in_specs += [pl.BlockSpec((tm, tn), lambda i, j: (i, j)), gate_spec]
    return pl.pallas_call(
        functools.partial(_out_proj_kernel, n_parts=n_parts),
        grid=(t // tm, d // tn),
        in_specs=in_specs,
        out_specs=pl.BlockSpec((tm, tn), lambda i, j: (i, j)),
        out_shape=jax.ShapeDtypeStruct((t, d), F32),
        compiler_params=_params("arbitrary", "arbitrary"),
        name="out_proj",
    )(*y_parts, *([w] * n_parts), x, gate_arr)


def _final_norm_kernel(x_ref, g_ref, o_ref):
    o_ref[...] = _rms(x_ref[...], g_ref[...])


def _final_norm(x, g, *, tm):
    t, d = x.shape
    return pl.pallas_call(
        _final_norm_kernel,
        grid=(t // tm,),
        in_specs=[pl.BlockSpec((tm, d), lambda i: (i, 0)), pl.BlockSpec((1, d), lambda i: (0, 0))],
        out_specs=pl.BlockSpec((tm, d), lambda i: (i, 0)),
        out_shape=jax.ShapeDtypeStruct((t, d), F32),
        compiler_params=_params("arbitrary"),
        name="final_norm",
    )(x, g)


def _in_proj_kernel(x_ref, sh_ref, sc_ref, g_ref, w_ref, *rest, has_lr):
    if has_lr:
        wlr_ref, o_ref, lr_ref, h_scr = rest
    else:
        o_ref, h_scr = rest

    @pl.when(pl.program_id(0) == 0)
    def _():
        h = _norm_modulate(x_ref[...], g_ref[0], sc_ref[...], sh_ref[...])
        h_scr[...] = h
        if has_lr:
            lr_ref[...] = _dot(h, wlr_ref[0]).astype(BF16)

    o_ref[...] = _dot(h_scr[...], w_ref[0]).astype(o_ref.dtype)


def _in_proj(x, mod_rows, norm_g3, w, wlr, *, layer):
    li = layer // 2
    t, d = x.shape
    tn = PROJ_TN
    n = w.shape[2] // tn * tn
    has_lr = wlr is not None
    in_specs = [
        pl.BlockSpec((t, d), lambda j: (0, 0)),
        pl.BlockSpec((None, None, t, d), lambda j: (layer, 0, 0, 0)),
        pl.BlockSpec((None, None, t, d), lambda j: (layer, 1, 0, 0)),
        pl.BlockSpec((1, 1, d), lambda j: (layer, 0, 0)),
        pl.BlockSpec((1, d, tn), lambda j: (li, 0, j)),
    ]
    args = [x, mod_rows, mod_rows, norm_g3, w]
    out_specs = [pl.BlockSpec((t, tn), lambda j: (0, j))]
    out_shape = [jax.ShapeDtypeStruct((t, n), BF16)]
    if has_lr:
        in_specs.append(pl.BlockSpec((1, d, LR_PAD), lambda j: (li, 0, 0)))
        args.append(wlr)
        out_specs.append(pl.BlockSpec((t, LR_PAD), lambda j: (0, 0)))
        out_shape.append(jax.ShapeDtypeStruct((t, LR_PAD), BF16))
    return pl.pallas_call(
        functools.partial(_in_proj_kernel, has_lr=has_lr),
        grid=(n // tn,),
        in_specs=in_specs,
        out_specs=out_specs,
        out_shape=out_shape,
        scratch_shapes=[pltpu.VMEM((t, d), BF16)],
        compiler_params=_params("arbitrary"),
        name="in_proj",
    )(*args)


def _even_mix_kernel(p_ref, cos_ref, sin_ref, wm_ref, bs_ref, lng_ref, retg_ref, s0_ref, *rest, ts, aliased):
    rest = rest[2:] if aliased else rest
    y_ref, sfin_ref, va_ref, dec_scr, qd_scr, kd_scr = rest

    @pl.when(pl.program_id(0) == 0)
    def _():
        _fill_retention_tables(dec_scr, qd_scr, kd_scr, ts)

    _zero_other_slots(sfin_ref)
    _zero_other_slots(va_ref)
    for a in range(A_HEADS):
        c0 = a * A_HEAD_DIM
        u = p_ref[:, c0:c0 + A_HEAD_DIM].astype(F32)
        v = p_ref[:, A_WIDTH + c0:A_WIDTH + c0 + A_HEAD_DIM].astype(F32)
        z = p_ref[:, 2 * A_WIDTH + c0:2 * A_WIDTH + c0 + A_HEAD_DIM].astype(F32)
        y, vn = _gmlp_unit(u, v, z, lng_ref[0, :, c0:c0 + A_HEAD_DIM], wm_ref[0, a], bs_ref[0, a], ts)
        va_ref[0, :, c0:c0 + A_HEAD_DIM] = vn
        y_ref[:, c0:c0 + A_HEAD_DIM] = y.astype(y_ref.dtype)

    q0 = 3 * A_WIDTH
    k0 = q0 + B_HEADS * B_DK
    v0 = k0 + B_HEADS * B_DK
    z0 = v0 + B_WIDTH
    cos = cos_ref[...]
    sin = sin_ref[...]
    for h in range(B_HEADS):
        hc = h * B_DK
        y, s_new = _retention_chunk(
            p_ref[:, q0 + hc:q0 + hc + B_DK].astype(F32), p_ref[:, k0 + hc:k0 + hc + B_DK].astype(F32),
            p_ref[:, v0 + hc:v0 + hc + B_DV].astype(F32), p_ref[:, z0 + hc:z0 + hc + B_DV].astype(F32),
            s0_ref[0, 0, h], cos, sin, dec_scr[h], qd_scr[h], kd_scr[h], retg_ref[0])
        sfin_ref[0, 0, h] = s_new
        y_ref[:, A_WIDTH + hc:A_WIDTH + hc + B_DV] = y.astype(y_ref.dtype)


def _even_mix(p, cos, sin, wm, bs, ln_g3, ret_g3, state, prev_state, prev_va, *, layer, batch, ts):
    li = layer // 2
    n_state = DEPTH // 2
    in_specs = [
        pl.BlockSpec((ts, E_IN), lambda b: (b, 0)),
        pl.BlockSpec((ts, B_DK // 2), lambda b: (0, 0)),
        pl.BlockSpec((ts, B_DK // 2), lambda b: (0, 0)),
        pl.BlockSpec((1, A_HEADS, ts, ts), lambda b: (li, 0, 0, 0)),
        pl.BlockSpec((1, A_HEADS, ts, A_HEAD_DIM), lambda b: (li, 0, 0, 0)),
        pl.BlockSpec((1, 1, A_WIDTH), lambda b: (li, 0, 0)),
        pl.BlockSpec((1, 1, B_DV), lambda b: (li, 0, 0)),
        pl.BlockSpec((1, 1, B_HEADS, B_DK, B_DV), lambda b: (li, b, 0, 0, 0)),
    ]
    args = [p, cos, sin, wm, bs, ln_g3, ret_g3, state]
    aliases = _alias_prev(in_specs, args, prev_state, 1)
    aliases.update(_alias_prev(in_specs, args, prev_va, 2))
    return pl.pallas_call(
        functools.partial(_even_mix_kernel, ts=ts, aliased=prev_state is not None),
        grid=(batch,),
        in_specs=in_specs,
        out_specs=[
            pl.BlockSpec((ts, E_MIX), lambda b: (b, 0)),
            _stacked_out_spec(n_state, li, prev_state is None, (1, B_HEADS, B_DK, B_DV), lambda b: (b, 0, 0, 0)),
            _stacked_out_spec(n_state, li, prev_va is None, (ts, A_WIDTH), lambda b: (b, 0)),
        ],
        out_shape=[
            jax.ShapeDtypeStruct((batch * ts, E_MIX), BF16),
            jax.ShapeDtypeStruct((n_state, batch, B_HEADS, B_DK, B_DV), F32),
            jax.ShapeDtypeStruct((n_state, batch * ts, A_WIDTH), F32),
        ],
        scratch_shapes=[
            pltpu.VMEM((B_HEADS, ts, ts), F32),
            pltpu.VMEM((B_HEADS, ts, B_DK // 2), F32),
            pltpu.VMEM((B_HEADS, ts, B_DK // 2), F32),
        ],
        input_output_aliases=aliases,
        compiler_params=_params("arbitrary"),
        name="even_mix",
    )(*args)


def _odd_mix_kernel(p_ref, lr_ref, wgk_ref, bgk_ref, glag_ref, s0_ref, *rest, ts, aliased):
    rest = rest[1:] if aliased else rest
    y_ref, sfin_ref = rest
    _zero_other_slots(sfin_ref)
    gk = _log_sigmoid(_dot(lr_ref[...], wgk_ref[0]) + bgk_ref[0]) * (1.0 / GLA_GATE_NORM)
    tril = _block_tril(ts, ts)
    causal = _causal(ts)
    k0 = C_HEADS * C_DK
    v0 = 2 * C_HEADS * C_DK
    z0 = v0 + C_WIDTH
    for h in range(C_HEADS):
        kc = h * C_DK
        vc = h * C_DV
        bcum = _cumsum_rows(gk[:, kc:kc + C_DK], tril)
        q = p_ref[:, kc:kc + C_DK].astype(F32) * (C_DK ** -0.5)
        k = p_ref[:, k0 + kc:k0 + kc + C_DK].astype(F32)
        v = p_ref[:, v0 + vc:v0 + vc + C_DV]
        z = p_ref[:, z0 + vc:z0 + vc + C_DV].astype(F32)
        y, st = _gla_chunk(q, k, v, z, bcum, s0_ref[0, 0, h].T, causal, glag_ref[0])
        sfin_ref[0, 0, h] = st.T
        y_ref[:, vc:vc + C_DV] = y.astype(y_ref.dtype)


def _odd_mix(p, lr, w_gk, b_gk3, gla_g3, state, prev_state, *, layer, batch, ts):
    li = layer // 2
    n_state = DEPTH // 2
    in_specs = [
        pl.BlockSpec((ts, O_MAIN), lambda b: (b, 0)),
        pl.BlockSpec((ts, LR_PAD), lambda b: (b, 0)),
        pl.BlockSpec((1, LR_PAD, C_HEADS * C_DK), lambda b: (li, 0, 0)),
        pl.BlockSpec((1, 1, C_HEADS * C_DK), lambda b: (li, 0, 0)),
        pl.BlockSpec((1, 1, C_DV), lambda b: (li, 0, 0)),
        pl.BlockSpec((1, 1, C_HEADS, C_DK, C_DV), lambda b: (li, b, 0, 0, 0)),
    ]
    args = [p, lr, w_gk, b_gk3, gla_g3, state]
    aliases = _alias_prev(in_specs, args, prev_state, 1)
    return pl.pallas_call(
        functools.partial(_odd_mix_kernel, ts=ts, aliased=prev_state is not None),
        grid=(batch,),
        in_specs=in_specs,
        out_specs=[
            pl.BlockSpec((ts, C_WIDTH), lambda b: (b, 0)),
            _stacked_out_spec(n_state, li, prev_state is None, (1, C_HEADS, C_DK, C_DV), lambda b: (b, 0, 0, 0)),
        ],
        out_shape=[
            jax.ShapeDtypeStruct((batch * ts, C_WIDTH), BF16),
            jax.ShapeDtypeStruct((n_state, batch, C_HEADS, C_DK, C_DV), F32),
        ],
        input_output_aliases=aliases,
        compiler_params=_params("arbitrary"),
        name="odd_mix",
    )(*args)


def _rope_tables(pos):
    half = B_DK // 2
    freqs = ROPE_BASE ** (-jnp.arange(half, dtype=F32) / half)
    ang = pos.astype(F32)[:, None] * freqs[None, :]
    return jnp.cos(ang), jnp.sin(ang)


def _gmlp_weights(gmlp_ws, gmlp_bs, seg):
    p = np.arange(seg)
    mask = (p[None, :] // CHUNK) <= (p[:, None] // CHUNK)
    wm = jnp.where(mask[None, None], gmlp_ws[:, :, :seg, :seg], 0.0).astype(BF16)
    bs = jnp.broadcast_to(gmlp_bs[:, :, :seg, None], gmlp_bs.shape[:2] + (seg, A_HEAD_DIM))
    return wm, bs


def _prompt_trunk(x, mod5, w, *, batch, length):
    cos, sin = _rope_tables(jnp.arange(length))
    wm, bs = _gmlp_weights(w["gmlp_ws"], w["gmlp_bs"], GMLP_CHUNK)
    ret, gla = None, None
    h = _first_norm(x, mod5, w["norm_g"], length=length)
    for layer in range(DEPTH):
        if layer % 2 == 0:
            ya, yb, ret = _fused_even(h, w["w_in_e"], cos, sin, wm, bs, w["gmlp_ln_g"], w["ret_norm_g"], ret,
                                      layer=layer, batch=batch, length=length)
            outs = _out_norm([ya, yb], w["w_out_e"], x, mod5, w["norm_g"], w["final_g"], layer=layer, length=length)
        else:
            y, gla = _fused_odd(h, w["w_in_o"], w["w_lr"], w["w_gk"], w["b_gk"], w["gla_norm_g"], gla,
                                layer=layer, batch=batch, length=length)
            outs = _out_norm([y], w["w_out_o"], x, mod5, w["norm_g"], w["final_g"], layer=layer, length=length)
        if layer < DEPTH - 1:
            x, h = outs
    return outs[0], ret, gla


def _sample_trunk(x, mod_rows, state_ret, state_gla, w, *, batch, length):
    t = batch * length
    cos, sin = _rope_tables(PAST_LEN + jnp.arange(length))
    wm, bs = _gmlp_weights(w["gmlp_ws"], w["gmlp_bs"], length)
    gate_spec = lambda layer: pl.BlockSpec((None, None, t, OUT_TN), lambda i, j: (layer, 2, 0, j))
    ret, gla, va = None, None, None
    for layer in range(DEPTH):
        if layer % 2 == 0:
            (p,) = _in_proj(x, mod_rows, w["norm_g"], w["w_in_e"], None, layer=layer)
            y, ret, va = _even_mix(p, cos, sin, wm, bs, w["gmlp_ln_g"], w["ret_norm_g"], state_ret, ret, va,
                                   layer=layer, batch=batch, ts=length)
            x = _out_proj([y], w["w_out_e"], x, mod_rows, gate_spec(layer), layer=layer, tm=t)
        else:
            p, lr = _in_proj(x, mod_rows, w["norm_g"], w["w_in_o"], w["w_lr"], layer=layer)
            y, gla = _odd_mix(p, lr, w["w_gk"], w["b_gk"], w["gla_norm_g"], state_gla, gla,
                              layer=layer, batch=batch, ts=length)
            x = _out_proj([y], w["w_out_o"], x, mod_rows, gate_spec(layer), layer=layer, tm=t)
    return _final_norm(x, w["final_g"], tm=t), ret, gla, va


def kernel(x_prompt, x_sample, state_ret, state_gla, c_prompt, c_sample, norm_g, w_mod, b_mod, w_in_e, gmlp_ln_g,
           gmlp_ws, gmlp_bs, ret_norm_g, w_out_e, w_in_o, w_gk2, b_gk, gla_norm_g, w_out_o, final_g):
    bp, lp, d = x_prompt.shape
    bs_, ls = x_sample.shape[:2]
    n_even, n_odd = w_in_e.shape[0], w_in_o.shape[0]

    w = {
        "norm_g": norm_g.reshape(DEPTH, 1, d),
        "w_in_e": w_in_e.astype(BF16),
        "w_out_e": w_out_e.astype(BF16),
        "w_in_o": w_in_o.astype(BF16),
        "w_lr": jnp.pad(w_in_o[:, :, O_MAIN:], ((0, 0), (0, 0), (0, LR_PAD - GLA_RANK))).astype(BF16),
        "w_gk": jnp.pad(w_gk2, ((0, 0), (0, LR_PAD - GLA_RANK), (0, 0))).astype(BF16),
        "w_out_o": w_out_o.astype(BF16),
        "gmlp_ws": gmlp_ws, "gmlp_bs": gmlp_bs,
        "gmlp_ln_g": gmlp_ln_g.reshape(n_even, 1, A_WIDTH),
        "ret_norm_g": ret_norm_g.reshape(n_even, 1, B_DV),
        "b_gk": b_gk.reshape(n_odd, 1, C_HEADS * C_DK),
        "gla_norm_g": gla_norm_g.reshape(n_odd, 1, C_DV),
        "final_g": final_g.reshape(1, d),
    }

    c_all = jnp.concatenate([c_prompt, c_sample], axis=0)
    mod = _modulation(c_all, w_mod, b_mod)
    mod5 = mod.reshape(DEPTH, bp + bs_, 3, 1, d)
    mod_rows = jnp.repeat(mod[:, bp:].reshape(DEPTH, bs_, 3, d).transpose(0, 2, 1, 3), ls, axis=2)

    y_p, ret_p, gla_p = _prompt_trunk(x_prompt.reshape(bp * lp, d), mod5, w, batch=bp, length=lp)
    y_s, ret_s, gla_s, va = _sample_trunk(x_sample.reshape(bs_ * ls, d), mod_rows, state_ret, state_gla, w,
                                          batch=bs_, length=ls)
    return (y_p.reshape(bp, lp, d), y_s.reshape(bs_, ls, d), ret_p, ret_s, gla_p, gla_s,
            va.reshape(n_even, bs_, ls, A_WIDTH))
```

```python
import functools
import math

import jax
import jax.numpy as jnp
import numpy as np
from jax import lax
from jax.experimental import pallas as pl
from jax.experimental.pallas import tpu as pltpu

F32 = jnp.float32
BF16 = jnp.bfloat16

D_MODEL = 2048
DEPTH = 4
PAST_LEN = 1024
EPS = 1e-6
CHUNK = 64
A_HEADS = 4
A_HEAD_DIM = 512
A_WIDTH = 2048
GMLP_CHUNK = 128
B_HEADS = 8
B_DK = 256
B_DV = 256
B_WIDTH = 2048
ROPE_BASE = 10000.0
C_HEADS = 8
C_DK = 256
C_DV = 512
C_WIDTH = 4096
GLA_RANK = 16
GLA_GATE_NORM = 16.0
E_IN = 3 * A_WIDTH + 2 * B_HEADS * B_DK + 2 * B_WIDTH
E_MIX = A_WIDTH + B_WIDTH
O_MAIN = 2 * C_HEADS * C_DK + 2 * C_WIDTH
LANE_TILE = 128
LR_PAD = LANE_TILE

VMEM_LIMIT_BYTES = 56 * 1024 * 1024

PROMPT_TM = 2048
PROMPT_RC = 256
PROMPT_GLA_BLK = 16
PROJ_TN = 1024
OUT_TN = 512
OUT_NORM_TM = 512
OUT_NORM_ROWS = 512


def _params(*sem):
    return pltpu.CompilerParams(dimension_semantics=sem, vmem_limit_bytes=VMEM_LIMIT_BYTES)


def _dot(a, b):
    return jnp.dot(a, b, preferred_element_type=F32)


def _dot_nt(a, b):
    return lax.dot_general(a, b, (((1,), (1,)), ((), ())), preferred_element_type=F32)


def _dot_tn(a, b):
    return lax.dot_general(a, b, (((0,), (0,)), ((), ())), preferred_element_type=F32)


def _silu(x):
    return x * jax.nn.sigmoid(x)


def _log_sigmoid(x):
    return jnp.minimum(x, 0.0) - jnp.log1p(jnp.exp(-jnp.abs(x)))


def _rms(x, g):
    return x * lax.rsqrt(jnp.mean(x * x, axis=-1, keepdims=True) + EPS) * g


def _norm_modulate(x, g, scale, shift):
    return (_rms(x, g) * (1.0 + scale) + shift).astype(BF16)


def _retention_log_gamma():
    return [math.log1p(-(2.0 ** (-5.0 - h))) for h in range(B_HEADS)]


def _fill_retention_tables(dec_scr, qd_scr, kd_scr, n):
    log_g = _retention_log_gamma()
    row = lax.broadcasted_iota(jnp.int32, (n, n), 0).astype(F32)
    col = lax.broadcasted_iota(jnp.int32, (n, n), 1).astype(F32)
    diff = row - col
    rowd = lax.broadcasted_iota(jnp.int32, (n, B_DK // 2), 0).astype(F32)
    for h in range(B_HEADS):
        dec_scr[h] = jnp.where(diff >= 0, jnp.exp(jnp.maximum(diff, 0.0) * log_g[h]), 0.0)
        qd_scr[h] = jnp.exp((rowd + 1.0) * log_g[h])
        kd_scr[h] = jnp.exp((n - 1.0 - rowd) * log_g[h])


def _rotary(xh, cos, sin):
    half = xh.shape[-1] // 2
    x1 = xh[:, :half]
    x2 = xh[:, half:]
    return jnp.concatenate([x1 * cos - x2 * sin, x1 * sin + x2 * cos], axis=-1)


def _run_next(fillers):
    if fillers:
        fillers.pop(0)()


def _gmlp_unit(u, v, z, g_ln, wm, bias, seg, fillers=()):
    fillers = list(fillers)
    v = jax.nn.gelu(v)
    vc = v - jnp.mean(v, axis=-1, keepdims=True)
    vn = vc * lax.rsqrt(jnp.mean(vc * vc, axis=-1, keepdims=True) + EPS) * g_ln
    vb = vn.astype(BF16)
    _run_next(fillers)
    mixed = [_dot(wm, vb[s * seg:(s + 1) * seg]) + bias for s in range(u.shape[0] // seg)]
    mixed = mixed[0] if len(mixed) == 1 else jnp.concatenate(mixed, axis=0)
    _run_next(fillers)
    y = jax.nn.gelu(u) * mixed * _silu(z)
    while fillers:
        _run_next(fillers)
    return y, vn


def _retention_chunk(q, k, v, z, s_old, cos, sin, dec, qd, kd, g_ret, fillers=()):
    fillers = list(fillers)
    n = q.shape[0]
    qd = jnp.concatenate([qd, qd], axis=-1)
    kd = jnp.concatenate([kd, kd], axis=-1)
    q = _rotary(q, cos, sin)
    k = _rotary(k, cos, sin) * (B_DK ** -0.5)
    vb = v.astype(BF16)
    _run_next(fillers)
    scores = _dot_nt(q.astype(BF16), k.astype(BF16)) * dec
    s_new = qd[n - 1:n, :] * s_old + _dot_tn((k * kd).astype(BF16), vb)
    _run_next(fillers)
    o = _dot(scores.astype(BF16), vb) + _dot((q * qd).astype(BF16), s_old.astype(BF16))
    _run_next(fillers)
    y = _rms(o, g_ret) * _silu(z)
    while fillers:
        _run_next(fillers)
    return y, s_new


def _gla_chunk(q, k, v, z, bcum, st_old, causal, g_gla):
    n = q.shape[0]
    bend = bcum[n - 1:n, :]
    qf = (q * jnp.exp(bcum)).astype(BF16)
    kf = (k * jnp.exp(-bcum)).astype(BF16)
    scores = jnp.where(causal, _dot_nt(qf, kf), 0.0).astype(BF16)
    o = _dot(scores, v) + _dot_nt(qf, st_old.astype(BF16))
    k_tail = (k * jnp.exp(bend - bcum)).astype(BF16)
    st_new = st_old * jnp.exp(bend) + _dot_tn(v, k_tail)
    return _rms(o, g_gla) * _silu(z), st_new


def _gla_chunk_anchored(q, k, v, z, gk, tril, st_old, g_gla, fine, fillers):
    fillers = list(fillers)
    n = q.shape[0]
    win = LANE_TILE
    bcum = _cumsum_rows(gk, tril)
    _run_next(fillers)
    bend = bcum[n - 1:n, :]
    score_rows = []
    for w0 in range(0, n, win):
        w1 = w0 + win
        kw, bw = k[w0:w1], bcum[w0:w1]
        row = lax.broadcasted_iota(jnp.int32, (win, C_DK), 0) + w0
        groups = []
        for lo in range(w0, w1, fine):
            hi = lo + fine
            anchor = bcum[lo - 1:lo, :] if lo else jnp.zeros_like(bend)
            qg = (q[lo:hi] * jnp.exp(bcum[lo:hi] - anchor)).astype(BF16)
            kg = (kw * jnp.exp(jnp.where(row < hi, anchor - bw, 0.0))).astype(BF16)
            t_idx = lax.broadcasted_iota(jnp.int32, (fine, win), 0) + lo
            s_idx = lax.broadcasted_iota(jnp.int32, (fine, win), 1) + w0
            groups.append(jnp.where(t_idx >= s_idx, _dot_nt(qg, kg), 0.0).astype(BF16))
        entries = [(jnp.concatenate(groups, axis=0), slice(w0, w1))]
        if w0:
            anchor = bcum[w0 - 1:w0, :]
            qw = (q[w0:w1] * jnp.exp(bw - anchor)).astype(BF16)
            kp = (k[:w0] * jnp.exp(anchor - bcum[:w0])).astype(BF16)
            entries.append((_dot_nt(qw, kp).astype(BF16), slice(0, w0)))
        score_rows.append(entries)
    q_state = (q * jnp.exp(bcum)).astype(BF16)
    k_tail = (k * jnp.exp(bend - bcum)).astype(BF16)
    _run_next(fillers)
    o_rows = []
    for entries in score_rows:
        acc = None
        for s, key_rows in entries:
            part = _dot(s, v[key_rows])
            acc = part if acc is None else acc + part
        o_rows.append(acc)
    o = jnp.concatenate(o_rows, axis=0) + _dot_nt(q_state, st_old.astype(BF16))
    st_new = st_old * jnp.exp(bend) + _dot_tn(v, k_tail)
    _run_next(fillers)
    y = _rms(o, g_gla) * _silu(z)
    while fillers:
        _run_next(fillers)
    return y, st_new


def _cumsum_rows(g, tril):
    g_hi = g.astype(BF16)
    g_lo = (g - g_hi.astype(F32)).astype(BF16)
    return _dot(tril, g_hi) + _dot(tril, g_lo)


def _block_tril(n, blk):
    row = lax.broadcasted_iota(jnp.int32, (n, n), 0)
    col = lax.broadcasted_iota(jnp.int32, (n, n), 1)
    return jnp.where((row >= col) & (row // blk == col // blk), 1.0, 0.0).astype(BF16)


def _causal(n):
    return lax.broadcasted_iota(jnp.int32, (n, n), 0) >= lax.broadcasted_iota(jnp.int32, (n, n), 1)


def _alias_prev(in_specs, args, prev, out_index):
    if prev is None:
        return {}
    in_specs.append(pl.BlockSpec(memory_space=pl.ANY))
    args.append(prev)
    return {len(args) - 1: out_index}


def _stacked_out_spec(n_stack, slot, first, tail_shape, tail_index):
    if first:
        return pl.BlockSpec((n_stack,) + tail_shape, lambda *g: (0,) + tail_index(*g))
    return pl.BlockSpec((1,) + tail_shape, lambda *g: (slot,) + tail_index(*g))


def _zero_other_slots(ref):
    if ref.shape[0] > 1:
        ref[1:] = jnp.zeros((ref.shape[0] - 1,) + ref.shape[1:], ref.dtype)


def _mod_kernel(c_ref, w_ref, b_ref, o_ref):
    s = _silu(c_ref[...]).astype(BF16)
    o_ref[0] = _dot(s, w_ref[0].astype(BF16)) + b_ref[0]


def _modulation(c_all, w_mod, b_mod):
    rows = c_all.shape[0]
    tn = PROJ_TN
    n = 3 * D_MODEL
    return pl.pallas_call(
        _mod_kernel,
        grid=(DEPTH, n // tn),
        in_specs=[
            pl.BlockSpec((rows, D_MODEL), lambda l, j: (0, 0)),
            pl.BlockSpec((1, D_MODEL, tn), lambda l, j: (l, 0, j)),
            pl.BlockSpec((1, 1, tn), lambda l, j: (l, 0, j)),
        ],
        out_specs=pl.BlockSpec((1, rows, tn), lambda l, j: (l, 0, j)),
        out_shape=jax.ShapeDtypeStruct((DEPTH, rows, n), F32),
        compiler_params=_params("arbitrary", "arbitrary"),
        name="modulation",
    )(c_all, w_mod, b_mod.reshape(DEPTH, 1, n))


def _fused_even_kernel(h_ref, wu_ref, wva_ref, wza_ref, wq_ref, wk_ref, wv_ref, wz_ref,
                       cos_ref, sin_ref, wm_ref, bs_ref, lng_ref, retg_ref, *rest, tm, rc, aliased):
    rest = rest[1:] if aliased else rest
    ya_ref, yb_ref, sfin_ref, s_scr, dec_scr, qd_scr, kd_scr = rest
    b = pl.program_id(0)
    j = pl.program_id(1)
    i = pl.program_id(2)

    @pl.when((b == 0) & (j == 0) & (i == 0))
    def _():
        _fill_retention_tables(dec_scr, qd_scr, kd_scr, rc)

    n_chunks = tm // rc

    def projection(c, w_refs):
        rows = pl.ds(pl.multiple_of(i * tm + c * rc, rc), rc)
        p = [None] * len(w_refs)

        def piece(n):
            def run():
                p[n] = _dot(h_ref[rows, :], w_refs[n][0])
            return run

        return p, [piece(n) for n in range(len(w_refs))]

    @pl.when(j < A_HEADS)
    def _():
        w_refs = (wva_ref, wu_ref, wza_ref)
        cur, pieces = projection(0, w_refs)
        while pieces:
            _run_next(pieces)
        for c in range(n_chunks):
            nxt, pieces = projection(c + 1, w_refs) if c + 1 < n_chunks else (None, [])
            y, _ = _gmlp_unit(cur[1], cur[0], cur[2], lng_ref[0], wm_ref[0, 0], bs_ref[0, 0], GMLP_CHUNK, pieces)
            ya_ref[c * rc:(c + 1) * rc, :] = y.astype(ya_ref.dtype)
            cur = nxt

    @pl.when((j >= A_HEADS) & (i == 0))
    def _():
        s_scr[...] = jnp.zeros_like(s_scr)

    @pl.when(j >= A_HEADS)
    def _():
        hd = j - A_HEADS
        s = s_scr[...]
        dec = dec_scr[hd]
        qd = qd_scr[hd]
        kd = kd_scr[hd]
        w_refs = (wq_ref, wk_ref, wv_ref, wz_ref)
        cur, pieces = projection(0, w_refs)
        while pieces:
            _run_next(pieces)
        for c in range(n_chunks):
            rows = slice(c * rc, (c + 1) * rc)
            nxt, pieces = projection(c + 1, w_refs) if c + 1 < n_chunks else (None, [])
            y, s = _retention_chunk(cur[0], cur[1], cur[2], cur[3], s, cos_ref[rows, :], sin_ref[rows, :],
                                    dec, qd, kd, retg_ref[0], pieces)
            yb_ref[rows, :] = y.astype(yb_ref.dtype)
            cur = nxt
        s_scr[...] = s
        sfin_ref[0, 0, 0] = s
        _zero_other_slots(sfin_ref)


def _fused_even(h, w_in, cos, sin, wm, bs, ln_g3, ret_g3, prev_state, *, layer, batch, length):
    li = layer // 2
    tm, rc = PROMPT_TM, PROMPT_RC
    tpb = length // tm
    t = batch * length
    n_items = A_HEADS + B_HEADS

    def a_of(j):
        return jnp.minimum(j, A_HEADS - 1)

    def h_of(j):
        return jnp.maximum(j - A_HEADS, 0)

    def ya_row(b, j, i):
        return b * tpb + jnp.where(j < A_HEADS, i, tpb - 1)

    def yb_row(b, j, i):
        return b * tpb + jnp.where(j >= A_HEADS, i, 0)

    def wa_spec(seg):
        return pl.BlockSpec((1, D_MODEL, A_HEAD_DIM), lambda b, j, i: (li, 0, seg * A_HEADS + a_of(j)))

    def wb_spec(seg):
        base = 3 * A_WIDTH // B_DK + seg * B_HEADS
        return pl.BlockSpec((1, D_MODEL, B_DK), lambda b, j, i: (li, 0, base + h_of(j)))

    rope_mode = pl.Buffered(1) if tpb == 1 else None
    in_specs = [
        pl.BlockSpec((length, D_MODEL), lambda b, j, i: (b, 0)),
        wa_spec(0), wa_spec(1), wa_spec(2),
        wb_spec(0), wb_spec(1), wb_spec(2), wb_spec(3),
        pl.BlockSpec((tm, B_DK // 2), lambda b, j, i: (i, 0), pipeline_mode=rope_mode),
        pl.BlockSpec((tm, B_DK // 2), lambda b, j, i: (i, 0), pipeline_mode=rope_mode),
        pl.BlockSpec((1, 1, GMLP_CHUNK, GMLP_CHUNK), lambda b, j, i: (li, a_of(j), 0, 0)),
        pl.BlockSpec((1, 1, GMLP_CHUNK, A_HEAD_DIM), lambda b, j, i: (li, a_of(j), 0, 0)),
        pl.BlockSpec((1, 1, A_HEAD_DIM), lambda b, j, i: (li, 0, a_of(j))),
        pl.BlockSpec((1, 1, B_DV), lambda b, j, i: (li, 0, 0)),
    ]
    args = [h, w_in, w_in, w_in, w_in, w_in, w_in, w_in, cos, sin, wm, bs, ln_g3, ret_g3]
    aliases = _alias_prev(in_specs, args, prev_state, 2)
    n_state = DEPTH // 2
    return pl.pallas_call(
        functools.partial(_fused_even_kernel, tm=tm, rc=rc, aliased=prev_state is not None),
        grid=(batch, n_items, tpb),
        in_specs=in_specs,
        out_specs=[
            pl.BlockSpec((tm, A_HEAD_DIM), lambda b, j, i: (ya_row(b, j, i), a_of(j))),
            pl.BlockSpec((tm, B_DV), lambda b, j, i: (yb_row(b, j, i), h_of(j))),
            _stacked_out_spec(n_state, li, prev_state is None, (1, 1, B_DK, B_DV),
                              lambda b, j, i: (b, h_of(j), 0, 0)),
        ],
        out_shape=[
            jax.ShapeDtypeStruct((t, A_WIDTH), BF16),
            jax.ShapeDtypeStruct((t, B_WIDTH), BF16),
            jax.ShapeDtypeStruct((n_state, batch, B_HEADS, B_DK, B_DV), F32),
        ],
        scratch_shapes=[
            pltpu.VMEM((B_DK, B_DV), F32),
            pltpu.VMEM((B_HEADS, rc, rc), F32),
            pltpu.VMEM((B_HEADS, rc, B_DK // 2), F32),
            pltpu.VMEM((B_HEADS, rc, B_DK // 2), F32),
        ],
        input_output_aliases=aliases,
        compiler_params=_params("arbitrary", "arbitrary", "arbitrary"),
        name="fused_even",
    )(*args)


def _fused_odd_kernel(h_ref, wq_ref, wk_ref, wv_ref, wz_ref, wlr_ref, wgk_ref, bgk_ref,
                      glag_ref, *rest, tm, rc, blk, aliased):
    rest = rest[1:] if aliased else rest
    y_ref, sfin_ref, lr_scr, st_scr = rest
    hd = pl.program_id(1)
    i = pl.program_id(2)

    @pl.when(hd == 0)
    def _():
        lr_scr[i] = _dot(h_ref[pl.ds(pl.multiple_of(i * tm, tm), tm), :], wlr_ref[0]).astype(BF16)

    @pl.when(i == 0)
    def _():
        st_scr[...] = jnp.zeros_like(st_scr)

    def projection(c):
        rows = slice(c * rc, (c + 1) * rc)
        grows = pl.ds(pl.multiple_of(i * tm + c * rc, rc), rc)
        p = {}

        def piece_q():
            p["q"] = _dot(h_ref[grows, :], wq_ref[0]) * (C_DK ** -0.5)
            p["gk"] = _log_sigmoid(_dot(lr_scr[i, rows, :], wgk_ref[0]) + bgk_ref[0]) * (1.0 / GLA_GATE_NORM)

        def piece_k():
            p["k"] = _dot(h_ref[grows, :], wk_ref[0])

        def piece_v():
            p["v"] = _dot(h_ref[grows, :], wv_ref[0]).astype(BF16)

        def piece_z():
            p["z"] = _dot(h_ref[grows, :], wz_ref[0])

        return p, [piece_v, piece_q, piece_z, piece_k]

    tril = _block_tril(rc, rc)
    st = st_scr[...]
    n_chunks = tm // rc
    cur, pieces = projection(0)
    while pieces:
        _run_next(pieces)
    for c in range(n_chunks):
        nxt, pieces = projection(c + 1) if c + 1 < n_chunks else (None, [])
        y, st = _gla_chunk_anchored(cur["q"], cur["k"], cur["v"], cur["z"], cur["gk"], tril, st, glag_ref[0], blk,
                                    pieces)
        y_ref[c * rc:(c + 1) * rc, :] = y.astype(y_ref.dtype)
        cur = nxt
    st_scr[...] = st
    sfin_ref[0, 0, 0] = st.T
    _zero_other_slots(sfin_ref)


def _fused_odd(h, w_in, w_lr, w_gk, b_gk3, gla_g3, prev_state, *, layer, batch, length):
    li = layer // 2
    tm, rc, blk = PROMPT_TM, PROMPT_RC, PROMPT_GLA_BLK
    tpb = length // tm
    t = batch * length
    v_base = 2 * C_HEADS * C_DK // C_DV

    in_specs = [
        pl.BlockSpec((length, D_MODEL), lambda b, h, i: (b, 0)),
        pl.BlockSpec((1, D_MODEL, C_DK), lambda b, h, i: (li, 0, h)),
        pl.BlockSpec((1, D_MODEL, C_DK), lambda b, h, i: (li, 0, C_HEADS + h)),
        pl.BlockSpec((1, D_MODEL, C_DV), lambda b, h, i: (li, 0, v_base + h)),
        pl.BlockSpec((1, D_MODEL, C_DV), lambda b, h, i: (li, 0, v_base + C_HEADS + h)),
        pl.BlockSpec((1, D_MODEL, LR_PAD), lambda b, h, i: (li, 0, 0)),
        pl.BlockSpec((1, LR_PAD, C_DK), lambda b, h, i: (li, 0, h)),
        pl.BlockSpec((1, 1, C_DK), lambda b, h, i: (li, 0, h)),
        pl.BlockSpec((1, 1, C_DV), lambda b, h, i: (li, 0, 0)),
    ]
    args = [h, w_in, w_in, w_in, w_in, w_lr, w_gk, b_gk3, gla_g3]
    aliases = _alias_prev(in_specs, args, prev_state, 1)
    n_state = DEPTH // 2
    return pl.pallas_call(
        functools.partial(_fused_odd_kernel, tm=tm, rc=rc, blk=blk, aliased=prev_state is not None),
        grid=(batch, C_HEADS, tpb),
        in_specs=in_specs,
        out_specs=[
            pl.BlockSpec((tm, C_DV), lambda b, h, i: (b * tpb + i, h)),
            _stacked_out_spec(n_state, li, prev_state is None, (1, 1, C_DK, C_DV), lambda b, h, i: (b, h, 0, 0)),
        ],
        out_shape=[
            jax.ShapeDtypeStruct((t, C_WIDTH), BF16),
            jax.ShapeDtypeStruct((n_state, batch, C_HEADS, C_DK, C_DV), F32),
        ],
        scratch_shapes=[
            pltpu.VMEM((tpb, tm, LR_PAD), BF16),
            pltpu.VMEM((C_DV, C_DK), F32),
        ],
        input_output_aliases=aliases,
        compiler_params=_params("arbitrary", "arbitrary", "arbitrary"),
        name="fused_odd",
    )(*args)


def _first_norm_kernel(x_ref, sh_ref, sc_ref, g_ref, h_ref):
    h_ref[...] = _norm_modulate(x_ref[...], g_ref[0], sc_ref[...], sh_ref[...])


def _first_norm(x, mod5, norm_g3, *, length):
    t, d = x.shape
    tm = OUT_NORM_TM
    tpb = length // tm

    def mod_spec(which):
        return pl.BlockSpec((None, None, None, 1, d), lambda i: (0, i // tpb, which, 0, 0))

    return pl.pallas_call(
        _first_norm_kernel,
        grid=(t // tm,),
        in_specs=[pl.BlockSpec((tm, d), lambda i: (i, 0)), mod_spec(0), mod_spec(1),
                  pl.BlockSpec((1, 1, d), lambda i: (0, 0, 0))],
        out_specs=pl.BlockSpec((tm, d), lambda i: (i, 0)),
        out_shape=jax.ShapeDtypeStruct((t, d), BF16),
        compiler_params=_params("arbitrary"),
        name="first_norm",
    )(x, mod5, mod5, norm_g3)


def _out_norm_kernel(*refs, n_parts, n_chunks, last):
    y_refs = refs[:n_parts]
    w_refs = refs[n_parts:2 * n_parts]
    if last:
        x_ref, gate_ref, g_ref, o_ref, xn_ref = refs[2 * n_parts:]
    else:
        x_ref, gate_ref, g_ref, sh_ref, sc_ref, xn_ref, h_ref = refs[2 * n_parts:]
    tm, d = x_ref.shape
    tn = d // n_chunks
    rg = OUT_NORM_ROWS
    ssq = [0.0] * (tm // rg)

    def residual(g, n, acc):
        rows, cols = slice(g * rg, (g + 1) * rg), slice(n * tn, (n + 1) * tn)
        xn = x_ref[rows, cols] + gate_ref[:, cols] * acc
        xn_ref[rows, cols] = xn
        ssq[g] = ssq[g] + jnp.sum(xn * xn, axis=-1, keepdims=True)

    def normalise(g):
        rows = slice(g * rg, (g + 1) * rg)
        r = lax.rsqrt(ssq[g] * (1.0 / d) + EPS)
        for n in range(n_chunks):
            cols = slice(n * tn, (n + 1) * tn)
            y = xn_ref[rows, cols] * r * g_ref[0][:, cols]
            if last:
                o_ref[rows, cols] = y
            else:
                h_ref[rows, cols] = (y * (1.0 + sc_ref[:, cols]) + sh_ref[:, cols]).astype(h_ref.dtype)

    pending = []
    for g in range(tm // rg):
        rows = slice(g * rg, (g + 1) * rg)
        for n in range(n_chunks):
            cols = slice(n * tn, (n + 1) * tn)
            acc = _dot(y_refs[0][rows, :], w_refs[0][0, :, cols])
            for p in range(1, n_parts):
                acc = acc + _dot(y_refs[p][rows, :], w_refs[p][0, :, cols])
            while pending:
                _run_next(pending)
            pending.append(functools.partial(residual, g, n, acc))
        pending.append(functools.partial(normalise, g))
    while pending:
        _run_next(pending)


def _out_norm(y_parts, w, x, mod5, norm_g3, final_g, *, layer, length):
    li = layer // 2
    last = layer == DEPTH - 1
    t, d = x.shape
    tm = OUT_NORM_TM
    tpb = length // tm
    n_parts = len(y_parts)
    kp = w.shape[1] // n_parts

    def mod_spec(lyr, which):
        return pl.BlockSpec((None, None, None, 1, d), lambda i: (lyr, i // tpb, which, 0, 0))

    row_spec = pl.BlockSpec((tm, d), lambda i: (i, 0))
    in_specs = [pl.BlockSpec((tm, kp), lambda i: (i, 0)) for _ in range(n_parts)]
    in_specs += [pl.BlockSpec((1, kp, d), functools.partial(lambda i, p: (li, p, 0), p=p),
                              pipeline_mode=pl.Buffered(1)) for p in range(n_parts)]
    in_specs += [row_spec, mod_spec(layer, 2)]
    args = [*y_parts, *([w] * n_parts), x, mod5]
    if last:
        in_specs.append(pl.BlockSpec((1, 1, d), lambda i: (0, 0, 0)))
        args.append(final_g.reshape(1, 1, d))
        out_specs = [row_spec]
        out_shape = [jax.ShapeDtypeStruct((t, d), F32)]
        scratch = [pltpu.VMEM((tm, d), F32)]
    else:
        in_specs += [pl.BlockSpec((1, 1, d), lambda i: (layer + 1, 0, 0)),
                     mod_spec(layer + 1, 0), mod_spec(layer + 1, 1)]
        args += [norm_g3, mod5, mod5]
        out_specs = [row_spec, row_spec]
        out_shape = [jax.ShapeDtypeStruct((t, d), F32), jax.ShapeDtypeStruct((t, d), BF16)]
        scratch = []
    return pl.pallas_call(
        functools.partial(_out_norm_kernel, n_parts=n_parts, n_chunks=d // OUT_TN, last=last),
        grid=(t // tm,),
        in_specs=in_specs,
        out_specs=out_specs,
        out_shape=out_shape,
        scratch_shapes=scratch,
        compiler_params=_params("arbitrary"),
        name="out_norm",
    )(*args)


def _out_proj_kernel(*refs, n_parts):
    y_refs = refs[:n_parts]
    w_refs = refs[n_parts:2 * n_parts]
    x_ref, gate_ref, o_ref = refs[2 * n_parts:]
    acc = _dot(y_refs[0][...], w_refs[0][0])
    for p in range(1, n_parts):
        acc = acc + _dot(y_refs[p][...], w_refs[p][0])
    o_ref[...] = x_ref[...] + gate_ref[...] * acc


def _out_proj(y_parts, w, x, gate_arr, gate_spec, *, layer, tm):
    li = layer // 2
    t, d = x.shape
    tn = OUT_TN
    n_parts = len(y_parts)
    kp = w.shape[1] // n_parts
    in_specs = [pl.BlockSpec((tm, kp), lambda i, j: (i, 0)) for _ in range(n_parts)]
    in_specs += [pl.BlockSpec((1, kp, tn), functools.partial(lambda i, j, p: (li, p, j), p=p)) for p in range(n_parts)]
    in_specs += [pl.BlockSpec((tm, tn), lambda i, j: (i, j)), gate_spec]
    return pl.pallas_call(
        functools.partial(_out_proj_kernel, n_parts=n_parts),
        grid=(t // tm, d // tn),
        in_specs=in_specs,
        out_specs=pl.BlockSpec((tm, tn), lambda i, j: (i, j)),
        out_shape=jax.ShapeDtypeStruct((t, d), F32),
        compiler_params=_params("arbitrary", "arbitrary"),
        name="out_proj",
    )(*y_parts, *([w] * n_parts), x, gate_arr)


def _final_norm_kernel(x_ref, g_ref, o_ref):
    o_ref[...] = _rms(x_ref[...], g_ref[...])


def _final_norm(x, g, *, tm):
    t, d = x.shape
    return pl.pallas_call(
        _final_norm_kernel,
        grid=(t // tm,),
        in_specs=[pl.BlockSpec((tm, d), lambda i: (i, 0)), pl.BlockSpec((1, d), lambda i: (0, 0))],
        out_specs=pl.BlockSpec((tm, d), lambda i: (i, 0)),
        out_shape=jax.ShapeDtypeStruct((t, d), F32),
        compiler_params=_params("arbitrary"),
        name="final_norm",
    )(x, g)


def _in_proj_kernel(x_ref, sh_ref, sc_ref, g_ref, w_ref, *rest, has_lr):
    if has_lr:
        wlr_ref, o_ref, lr_ref, h_scr = rest
    else:
        o_ref, h_scr = rest

    @pl.when(pl.program_id(0) == 0)
    def _():
        h = _norm_modulate(x_ref[...], g_ref[0], sc_ref[...], sh_ref[...])
        h_scr[...] = h
        if has_lr:
            lr_ref[...] = _dot(h, wlr_ref[0]).astype(BF16)

    o_ref[...] = _dot(h_scr[...], w_ref[0]).astype(o_ref.dtype)


def _in_proj(x, mod_rows, norm_g3, w, wlr, *, layer):
    li = layer // 2
    t, d = x.shape
    tn = PROJ_TN
    n = w.shape[2] // tn * tn
    has_lr = wlr is not None
    in_specs = [
        pl.BlockSpec((t, d), lambda j: (0, 0)),
        pl.BlockSpec((None, None, t, d), lambda j: (layer, 0, 0, 0)),
        pl.BlockSpec((None, None, t, d), lambda j: (layer, 1, 0, 0)),
        pl.BlockSpec((1, 1, d), lambda j: (layer, 0, 0)),
        pl.BlockSpec((1, d, tn), lambda j: (li, 0, j)),
    ]
    args = [x, mod_rows, mod_rows, norm_g3, w]
    out_specs = [pl.BlockSpec((t, tn), lambda j: (0, j))]
    out_shape = [jax.ShapeDtypeStruct((t, n), BF16)]
    if has_lr:
        in_specs.append(pl.BlockSpec((1, d, LR_PAD), lambda j: (li, 0, 0)))
        args.append(wlr)
        out_specs.append(pl.BlockSpec((t, LR_PAD), lambda j: (0, 0)))
        out_shape.append(jax.ShapeDtypeStruct((t, LR_PAD), BF16))
    return pl.pallas_call(
        functools.partial(_in_proj_kernel, has_lr=has_lr),
        grid=(n // tn,),
        in_specs=in_specs,
        out_specs=out_specs,
        out_shape=out_shape,
        scratch_shapes=[pltpu.VMEM((t, d), BF16)],
        compiler_params=_params("arbitrary"),
        name="in_proj",
    )(*args)


def _even_mix_kernel(p_ref, cos_ref, sin_ref, wm_ref, bs_ref, lng_ref, retg_ref, s0_ref, *rest, ts, aliased):
    rest = rest[2:] if aliased else rest
    y_ref, sfin_ref, va_ref, dec_scr, qd_scr, kd_scr = rest

    @pl.when(pl.program_id(0) == 0)
    def _():
        _fill_retention_tables(dec_scr, qd_scr, kd_scr, ts)

    _zero_other_slots(sfin_ref)
    _zero_other_slots(va_ref)
    for a in range(A_HEADS):
        c0 = a * A_HEAD_DIM
        u = p_ref[:, c0:c0 + A_HEAD_DIM].astype(F32)
        v = p_ref[:, A_WIDTH + c0:A_WIDTH + c0 + A_HEAD_DIM].astype(F32)
        z = p_ref[:, 2 * A_WIDTH + c0:2 * A_WIDTH + c0 + A_HEAD_DIM].astype(F32)
        y, vn = _gmlp_unit(u, v, z, lng_ref[0, :, c0:c0 + A_HEAD_DIM], wm_ref[0, a], bs_ref[0, a], ts)
        va_ref[0, :, c0:c0 + A_HEAD_DIM] = vn
        y_ref[:, c0:c0 + A_HEAD_DIM] = y.astype(y_ref.dtype)

    q0 = 3 * A_WIDTH
    k0 = q0 + B_HEADS * B_DK
    v0 = k0 + B_HEADS * B_DK
    z0 = v0 + B_WIDTH
    cos = cos_ref[...]
    sin = sin_ref[...]
    for h in range(B_HEADS):
        hc = h * B_DK
        y, s_new = _retention_chunk(
            p_ref[:, q0 + hc:q0 + hc + B_DK].astype(F32), p_ref[:, k0 + hc:k0 + hc + B_DK].astype(F32),
            p_ref[:, v0 + hc:v0 + hc + B_DV].astype(F32), p_ref[:, z0 + hc:z0 + hc + B_DV].astype(F32),
            s0_ref[0, 0, h], cos, sin, dec_scr[h], qd_scr[h], kd_scr[h], retg_ref[0])
        sfin_ref[0, 0, h] = s_new
        y_ref[:, A_WIDTH + hc:A_WIDTH + hc + B_DV] = y.astype(y_ref.dtype)


def _even_mix(p, cos, sin, wm, bs, ln_g3, ret_g3, state, prev_state, prev_va, *, layer, batch, ts):
    li = layer // 2
    n_state = DEPTH // 2
    in_specs = [
        pl.BlockSpec((ts, E_IN), lambda b: (b, 0)),
        pl.BlockSpec((ts, B_DK // 2), lambda b: (0, 0)),
        pl.BlockSpec((ts, B_DK // 2), lambda b: (0, 0)),
        pl.BlockSpec((1, A_HEADS, ts, ts), lambda b: (li, 0, 0, 0)),
        pl.BlockSpec((1, A_HEADS, ts, A_HEAD_DIM), lambda b: (li, 0, 0, 0)),
        pl.BlockSpec((1, 1, A_WIDTH), lambda b: (li, 0, 0)),
        pl.BlockSpec((1, 1, B_DV), lambda b: (li, 0, 0)),
        pl.BlockSpec((1, 1, B_HEADS, B_DK, B_DV), lambda b: (li, b, 0, 0, 0)),
    ]
    args = [p, cos, sin, wm, bs, ln_g3, ret_g3, state]
    aliases = _alias_prev(in_specs, args, prev_state, 1)
    aliases.update(_alias_prev(in_specs, args, prev_va, 2))
    return pl.pallas_call(
        functools.partial(_even_mix_kernel, ts=ts, aliased=prev_state is not None),
        grid=(batch,),
        in_specs=in_specs,
        out_specs=[
            pl.BlockSpec((ts, E_MIX), lambda b: (b, 0)),
            _stacked_out_spec(n_state, li, prev_state is None, (1, B_HEADS, B_DK, B_DV), lambda b: (b, 0, 0, 0)),
            _stacked_out_spec(n_state, li, prev_va is None, (ts, A_WIDTH), lambda b: (b, 0)),
        ],
        out_shape=[
            jax.ShapeDtypeStruct((batch * ts, E_MIX), BF16),
            jax.ShapeDtypeStruct((n_state, batch, B_HEADS, B_DK, B_DV), F32),
            jax.ShapeDtypeStruct((n_state, batch * ts, A_WIDTH), F32),
        ],
        scratch_shapes=[
            pltpu.VMEM((B_HEADS, ts, ts), F32),
            pltpu.VMEM((B_HEADS, ts, B_DK // 2), F32),
            pltpu.VMEM((B_HEADS, ts, B_DK // 2), F32),
        ],
        input_output_aliases=aliases,
        compiler_params=_params("arbitrary"),
        name="even_mix",
    )(*args)


def _odd_mix_kernel(p_ref, lr_ref, wgk_ref, bgk_ref, glag_ref, s0_ref, *rest, ts, aliased):
    rest = rest[1:] if aliased else rest
    y_ref, sfin_ref = rest
    _zero_other_slots(sfin_ref)
    gk = _log_sigmoid(_dot(lr_ref[...], wgk_ref[0]) + bgk_ref[0]) * (1.0 / GLA_GATE_NORM)
    tril = _block_tril(ts, ts)
    causal = _causal(ts)
    k0 = C_HEADS * C_DK
    v0 = 2 * C_HEADS * C_DK
    z0 = v0 + C_WIDTH
    for h in range(C_HEADS):
        kc = h * C_DK
        vc = h * C_DV
        bcum = _cumsum_rows(gk[:, kc:kc + C_DK], tril)
        q = p_ref[:, kc:kc + C_DK].astype(F32) * (C_DK ** -0.5)
        k = p_ref[:, k0 + kc:k0 + kc + C_DK].astype(F32)
        v = p_ref[:, v0 + vc:v0 + vc + C_DV]
        z = p_ref[:, z0 + vc:z0 + vc + C_DV].astype(F32)
        y, st = _gla_chunk(q, k, v, z, bcum, s0_ref[0, 0, h].T, causal, glag_ref[0])
        sfin_ref[0, 0, h] = st.T
        y_ref[:, vc:vc + C_DV] = y.astype(y_ref.dtype)


def _odd_mix(p, lr, w_gk, b_gk3, gla_g3, state, prev_state, *, layer, batch, ts):
    li = layer // 2
    n_state = DEPTH // 2
    in_specs = [
        pl.BlockSpec((ts, O_MAIN), lambda b: (b, 0)),
        pl.BlockSpec((ts, LR_PAD), lambda b: (b, 0)),
        pl.BlockSpec((1, LR_PAD, C_HEADS * C_DK), lambda b: (li, 0, 0)),
        pl.BlockSpec((1, 1, C_HEADS * C_DK), lambda b: (li, 0, 0)),
        pl.BlockSpec((1, 1, C_DV), lambda b: (li, 0, 0)),
        pl.BlockSpec((1, 1, C_HEADS, C_DK, C_DV), lambda b: (li, b, 0, 0, 0)),
    ]
    args = [p, lr, w_gk, b_gk3, gla_g3, state]
    aliases = _alias_prev(in_specs, args, prev_state, 1)
    return pl.pallas_call(
        functools.partial(_odd_mix_kernel, ts=ts, aliased=prev_state is not None),
        grid=(batch,),
        in_specs=in_specs,
        out_specs=[
            pl.BlockSpec((ts, C_WIDTH), lambda b: (b, 0)),
            _stacked_out_spec(n_state, li, prev_state is None, (1, C_HEADS, C_DK, C_DV), lambda b: (b, 0, 0, 0)),
        ],
        out_shape=[
            jax.ShapeDtypeStruct((batch * ts, C_WIDTH), BF16),
            jax.ShapeDtypeStruct((n_state, batch, C_HEADS, C_DK, C_DV), F32),
        ],
        input_output_aliases=aliases,
        compiler_params=_params("arbitrary"),
        name="odd_mix",
    )(*args)


def _rope_tables(pos):
    half = B_DK // 2
    freqs = ROPE_BASE ** (-jnp.arange(half, dtype=F32) / half)
    ang = pos.astype(F32)[:, None] * freqs[None, :]
    return jnp.cos(ang), jnp.sin(ang)


def _gmlp_weights(gmlp_ws, gmlp_bs, seg):
    p = np.arange(seg)
    mask = (p[None, :] // CHUNK) <= (p[:, None] // CHUNK)
    wm = jnp.where(mask[None, None], gmlp_ws[:, :, :seg, :seg], 0.0).astype(BF16)
    bs = jnp.broadcast_to(gmlp_bs[:, :, :seg, None], gmlp_bs.shape[:2] + (seg, A_HEAD_DIM))
    return wm, bs


def _prompt_trunk(x, mod5, w, *, batch, length):
    cos, sin = _rope_tables(jnp.arange(length))
    wm, bs = _gmlp_weights(w["gmlp_ws"], w["gmlp_bs"], GMLP_CHUNK)
    ret, gla = None, None
    h = _first_norm(x, mod5, w["norm_g"], length=length)
    for layer in range(DEPTH):
        if layer % 2 == 0:
            ya, yb, ret = _fused_even(h, w["w_in_e"], cos, sin, wm, bs, w["gmlp_ln_g"], w["ret_norm_g"], ret,
                                      layer=layer, batch=batch, length=length)
            outs = _out_norm([ya, yb], w["w_out_e"], x, mod5, w["norm_g"], w["final_g"], layer=layer, length=length)
        else:
            y, gla = _fused_odd(h, w["w_in_o"], w["w_lr"], w["w_gk"], w["b_gk"], w["gla_norm_g"], gla,
                                layer=layer, batch=batch, length=length)
            outs = _out_norm([y], w["w_out_o"], x, mod5, w["norm_g"], w["final_g"], layer=layer, length=length)
        if layer < DEPTH - 1:
            x, h = outs
    return outs[0], ret, gla


def _sample_trunk(x, mod_rows, state_ret, state_gla, w, *, batch, length):
    t = batch * length
    cos, sin = _rope_tables(PAST_LEN + jnp.arange(length))
    wm, bs = _gmlp_weights(w["gmlp_ws"], w["gmlp_bs"], length)
    gate_spec = lambda layer: pl.BlockSpec((None, None, t, OUT_TN), lambda i, j: (layer, 2, 0, j))
    ret, gla, va = None, None, None
    for layer in range(DEPTH):
        if layer % 2 == 0:
            (p,) = _in_proj(x, mod_rows, w["norm_g"], w["w_in_e"], None, layer=layer)
            y, ret, va = _even_mix(p, cos, sin, wm, bs, w["gmlp_ln_g"], w["ret_norm_g"], state_ret, ret, va,
                                   layer=layer, batch=batch, ts=length)
            x = _out_proj([y], w["w_out_e"], x, mod_rows, gate_spec(layer), layer=layer, tm=t)
        else:
            p, lr = _in_proj(x, mod_rows, w["norm_g"], w["w_in_o"], w["w_lr"], layer=layer)
            y, gla = _odd_mix(p, lr, w["w_gk"], w["b_gk"], w["gla_norm_g"], state_gla, gla,
                              layer=layer, batch=batch, ts=length)
            x = _out_proj([y], w["w_out_o"], x, mod_rows, gate_spec(layer), layer=layer, tm=t)
    return _final_norm(x, w["final_g"], tm=t), ret, gla, va


def kernel(x_prompt, x_sample, state_ret, state_gla, c_prompt, c_sample, norm_g, w_mod, b_mod, w_in_e, gmlp_ln_g,
           gmlp_ws, gmlp_bs, ret_norm_g, w_out_e, w_in_o, w_gk2, b_gk, gla_norm_g, w_out_o, final_g):
    bp, lp, d = x_prompt.shape
    bs_, ls = x_sample.shape[:2]
    n_even, n_odd = w_in_e.shape[0], w_in_o.shape[0]

    w = {
        "norm_g": norm_g.reshape(DEPTH, 1, d),
        "w_in_e": w_in_e.astype(BF16),
        "w_out_e": w_out_e.astype(BF16),
        "w_in_o": w_in_o.astype(BF16),
        "w_lr": jnp.pad(w_in_o[:, :, O_MAIN:], ((0, 0), (0, 0), (0, LR_PAD - GLA_RANK))).astype(BF16),
        "w_gk": jnp.pad(w_gk2, ((0, 0), (0, LR_PAD - GLA_RANK), (0, 0))).astype(BF16),
        "w_out_o": w_out_o.astype(BF16),
        "gmlp_ws": gmlp_ws, "gmlp_bs": gmlp_bs,
        "gmlp_ln_g": gmlp_ln_g.reshape(n_even, 1, A_WIDTH),
        "ret_norm_g": ret_norm_g.reshape(n_even, 1, B_DV),
        "b_gk": b_gk.reshape(n_odd, 1, C_HEADS * C_DK),
        "gla_norm_g": gla_norm_g.reshape(n_odd, 1, C_DV),
        "final_g": final_g.reshape(1, d),
    }

    c_all = jnp.concatenate([c_prompt, c_sample], axis=0)
    mod = _modulation(c_all, w_mod, b_mod)
    mod5 = mod.reshape(DEPTH, bp + bs_, 3, 1, d)
    mod_rows = jnp.repeat(mod[:, bp:].reshape(DEPTH, bs_, 3, d).transpose(0, 2, 1, 3), ls, axis=2)

    y_p, ret_p, gla_p = _prompt_trunk(x_prompt.reshape(bp * lp, d), mod5, w, batch=bp, length=lp)
    y_s, ret_s, gla_s, va = _sample_trunk(x_sample.reshape(bs_ * ls, d), mod_rows, state_ret, state_gla, w,
                                          batch=bs_, length=ls)
    return (y_p.reshape(bp, lp, d), y_s.reshape(bs_, ls, d), ret_p, ret_s, gla_p, gla_s,
            va.reshape(n_even, bs_, ls, A_WIDTH))
```
